```python
import math
import jax, jax.numpy as jnp
from jax import lax
import numpy as np

D_MODEL = 4096
BATCH = 2
SEQ = 4096
DEPTH = 1
DEC_BATCH = 16
DEC_SEQ = 16
PAST_LEN = 4096

CHUNK = 64
N_HEADS = 16
HEAD_DIM = D_MODEL // (2 * N_HEADS)
QK_WIDTH = N_HEADS * 2 * HEAD_DIM
V_WIDTH = N_HEADS * 2 * HEAD_DIM
SGU_CHUNK = 128
SGU_GROUPS = 8
SGU_WIDTH = D_MODEL
SGU_GROUP_DIM = SGU_WIDTH // SGU_GROUPS
D_FF = 4 * D_MODEL
ROPE_THETA = 10000.0
EPS = 1e-6
Q_BLOCK = 128
NEG_INF = -1e30
IN_WIDTH = 2 * QK_WIDTH + V_WIDTH + 2 * SGU_WIDTH + 2 * D_MODEL
SPLITS = (QK_WIDTH, 2 * QK_WIDTH, 2 * QK_WIDTH + V_WIDTH,
          2 * QK_WIDTH + V_WIDTH + SGU_WIDTH, 2 * QK_WIDTH + V_WIDTH + 2 * SGU_WIDTH)

kernel_name = 'hybrid_diffattn_sgu_streaming_step'


def rmsnorm(x, g):
    xf = x.astype(jnp.float32)
    y = xf * lax.rsqrt(jnp.mean(xf * xf, axis=-1, keepdims=True) + EPS)
    return (y * g.astype(jnp.float32)).astype(x.dtype)


def rope(x, pos):
    half = HEAD_DIM // 2
    inv = ROPE_THETA ** (-jnp.arange(half, dtype=jnp.float32) / half)
    ang = pos.astype(jnp.float32)[:, None] * inv[None, :]
    cos = jnp.cos(ang)[:, None, None, :]
    sin = jnp.sin(ang)[:, None, None, :]
    xf = x.astype(jnp.float32)
    x1, x2 = xf[..., :half], xf[..., half:]
    out = jnp.concatenate([x1 * cos - x2 * sin, x2 * cos + x1 * sin], axis=-1)
    return out.astype(x.dtype)


def diff_attn(q, q_pos, segs, lam):
    scale = HEAD_DIM ** -0.5
    scores = []
    for k, _, k_pos in segs:
        s = jnp.einsum('bqhcd,bkhcd->bhcqk', q, k).astype(jnp.float32) * scale
        vis = (k_pos[None, :] // CHUNK) <= (q_pos[:, None] // CHUNK)
        scores.append(jnp.where(vis, s, NEG_INF))
    p = jax.nn.softmax(jnp.concatenate(scores, axis=-1), axis=-1)
    a = p[:, :, 0] - lam * p[:, :, 1]
    o = None
    off = 0
    for k, v, _ in segs:
        n = k.shape[1]
        part = jnp.einsum('bhqk,bkhe->bqhe', a[..., off:off + n].astype(v.dtype), v)
        o = part if o is None else o + part
        off += n
    return o


def _layer(x, pos, k_past, v_past, past_pos, lambda_init, norm_mix, w_in, b_gate, q_norm, k_norm,
           lambda_q1, lambda_k1, lambda_q2, lambda_k2, subln, sgu_norm, w_s, b_s, w_o,
           norm_ffn, w_up, w_down):
    B, T, _ = x.shape
    h = rmsnorm(x, norm_mix)
    z = jnp.einsum('btd,de->bte', h, w_in)
    q, k, v, u, vs, zg = jnp.split(z, SPLITS, axis=-1)

    q = rope(rmsnorm(q.reshape(B, T, N_HEADS, 2, HEAD_DIM), q_norm), pos)
    k = rope(rmsnorm(k.reshape(B, T, N_HEADS, 2, HEAD_DIM), k_norm), pos)
    v = v.reshape(B, T, N_HEADS, 2 * HEAD_DIM)
    lam = (jnp.exp(jnp.sum(lambda_q1.astype(jnp.float32) * lambda_k1.astype(jnp.float32)))
           - jnp.exp(jnp.sum(lambda_q2.astype(jnp.float32) * lambda_k2.astype(jnp.float32)))
           + lambda_init)
    if k_past is None:
        nb = T // Q_BLOCK
        qb = jnp.swapaxes(q.reshape(B, nb, Q_BLOCK, N_HEADS, 2, HEAD_DIM), 0, 1)
        pb = pos.reshape(nb, Q_BLOCK)

        def blk(args):
            qq, pp = args
            return diff_attn(qq, pp, ((k, v, pos),), lam)

        o = lax.map(blk, (qb, pb))
        o = jnp.swapaxes(o, 0, 1).reshape(B, T, N_HEADS, 2 * HEAD_DIM)
    else:
        P = k_past.shape[1]
        kp = k_past.reshape(B, P, N_HEADS, 2, HEAD_DIM)
        o = diff_attn(q, pos, ((kp, v_past, past_pos), (k, v, pos)), lam)
    o_attn = (rmsnorm(o, subln) * (1.0 - lambda_init)).reshape(B, T, V_WIDTH)

    vs = rmsnorm(vs, sgu_norm)
    L = min(T, SGU_CHUNK)
    vc = vs.reshape(B, T // L, L, SGU_GROUPS, SGU_GROUP_DIM)
    w = jnp.tril(w_s[:, :L, :L])
    s = jnp.einsum('gts,bnsgc->bntgc', w, vc) + jnp.swapaxes(b_s[:, :L], 0, 1)[:, :, None]
    o_sgu = u * s.reshape(B, T, SGU_WIDTH)

    g = jax.nn.sigmoid(zg + b_gate)
    merged = g[..., :D_MODEL] * o_attn + g[..., D_MODEL:] * o_sgu
    x1 = x + jnp.einsum('btd,de->bte', merged, w_o)

    hf = rmsnorm(x1, norm_ffn)
    y = x1 + jnp.einsum('btf,fd->btd', jnp.square(jax.nn.relu(jnp.einsum('btd,df->btf', hf, w_up))), w_down)
    return y, k.reshape(B, T, N_HEADS, 2 * HEAD_DIM), v, vs


def setup_inputs(seed: int = 0) -> dict:
    key = jax.random.key(seed)
    ks = jax.random.split(key, 24)
    f = jnp.float32
    nrm = lambda k, shape, sc: jax.random.normal(k, shape, f) * sc
    return {
        'x_prompt': nrm(ks[0], (BATCH, SEQ, D_MODEL), 1.0),
        'x_sample': nrm(ks[1], (DEC_BATCH, DEC_SEQ, D_MODEL), 1.0),
        'cache_k_attn': nrm(ks[2], (DEPTH, DEC_BATCH, PAST_LEN, N_HEADS, 2 * HEAD_DIM), 1.0),
        'cache_v_attn': nrm(ks[3], (DEPTH, DEC_BATCH, PAST_LEN, N_HEADS, 2 * HEAD_DIM), 1.0),
        'norm_mix': 1.0 + nrm(ks[4], (DEPTH, D_MODEL), 0.02),
        'w_in': nrm(ks[5], (DEPTH, D_MODEL, IN_WIDTH), D_MODEL ** -0.5),
        'b_gate': nrm(ks[6], (DEPTH, 2 * D_MODEL), 0.02),
        'q_norm': 1.0 + nrm(ks[7], (DEPTH, HEAD_DIM), 0.02),
        'k_norm': 1.0 + nrm(ks[8], (DEPTH, HEAD_DIM), 0.02),
        'lambda_q1': nrm(ks[9], (DEPTH, HEAD_DIM), 0.1),
        'lambda_k1': nrm(ks[10], (DEPTH, HEAD_DIM), 0.1),
        'lambda_q2': nrm(ks[11], (DEPTH, HEAD_DIM), 0.1),
        'lambda_k2': nrm(ks[12], (DEPTH, HEAD_DIM), 0.1),
        'subln': 1.0 + nrm(ks[13], (DEPTH, 2 * HEAD_DIM), 0.02),
        'sgu_norm': 1.0 + nrm(ks[14], (DEPTH, SGU_WIDTH), 0.02),
        'w_s': nrm(ks[15], (DEPTH, SGU_GROUPS, SGU_CHUNK, SGU_CHUNK), SGU_CHUNK ** -0.5),
        'b_s': 1.0 + nrm(ks[16], (DEPTH, SGU_GROUPS, SGU_CHUNK), 0.02),
        'w_o': nrm(ks[17], (DEPTH, D_MODEL, D_MODEL), D_MODEL ** -0.5),
        'norm_ffn': 1.0 + nrm(ks[18], (DEPTH, D_MODEL), 0.02),
        'w_up': nrm(ks[19], (DEPTH, D_MODEL, D_FF), D_MODEL ** -0.5),
        'w_down': nrm(ks[20], (DEPTH, D_FF, D_MODEL), D_FF ** -0.5),
    }


def reference(x_prompt, x_sample, cache_k_attn, cache_v_attn, norm_mix, w_in, b_gate, q_norm, k_norm,
              lambda_q1, lambda_k1, lambda_q2, lambda_k2, subln, sgu_norm, w_s, b_s, w_o,
              norm_ffn, w_up, w_down):
    pos_p = jnp.arange(SEQ, dtype=jnp.int32)
    pos_s = PAST_LEN + jnp.arange(DEC_SEQ, dtype=jnp.int32)
    past_pos = jnp.arange(PAST_LEN, dtype=jnp.int32)
    xp, xs = x_prompt, x_sample
    kp_l, vp_l, ks_l, vs_l, sg_l = [], [], [], [], []
    for l in range(DEPTH):
        lambda_init = 0.8 - 0.6 * math.exp(-0.3 * l)
        wl = (norm_mix[l], w_in[l], b_gate[l], q_norm[l], k_norm[l], lambda_q1[l], lambda_k1[l],
              lambda_q2[l], lambda_k2[l], subln[l], sgu_norm[l], w_s[l], b_s[l], w_o[l],
              norm_ffn[l], w_up[l], w_down[l])
        xp, kp, vp, _ = _layer(xp, pos_p, None, None, None, lambda_init, *wl)
        xs, ksm, vsm, sgv = _layer(xs, pos_s, cache_k_attn[l], cache_v_attn[l], past_pos, lambda_init, *wl)
        kp_l.append(kp); vp_l.append(vp); ks_l.append(ksm); vs_l.append(vsm); sg_l.append(sgv)
    new_k_prompt = jnp.stack(kp_l)
    new_v_prompt = jnp.stack(vp_l)
    new_k_sample = jnp.stack(ks_l)
    new_v_sample = jnp.stack(vs_l)
    new_sgu_v_sample = jnp.stack(sg_l)
    return (xp, xs, new_k_prompt, new_v_prompt, new_k_sample, new_v_sample, new_sgu_v_sample)
```

```python
import functools
import math

import jax
import jax.numpy as jnp
from jax import lax
from jax.experimental import pallas as pl
from jax.experimental.pallas import tpu as pltpu

D_MODEL = 4096
BATCH = 2
SEQ = 4096
DEC_BATCH = 16
DEC_SEQ = 16
PAST_LEN = 4096
CHUNK = 64
N_HEADS = 16
HEAD_DIM = 128
SGU_CHUNK = 128
SGU_GROUPS = 8
D_FF = 4 * D_MODEL
ROPE_THETA = 10000.0
EPS = 1e-6
NEG_INF = -1e30

LANES = 128
VMEM_LIMIT_BYTES = 56 * 1024 * 1024
MAX_TK = 4096
F32 = jnp.float32
BF16 = jnp.bfloat16


def _params(*sem):
    return pltpu.CompilerParams(dimension_semantics=sem, vmem_limit_bytes=VMEM_LIMIT_BYTES)


def _rmsnorm_kernel(x_ref, g_ref, o_ref):
    x = x_ref[...]
    ms = jnp.mean(x * x, axis=-1, keepdims=True)
    o_ref[...] = (x * lax.rsqrt(ms + EPS) * g_ref[...]).astype(o_ref.dtype)


def _rmsnorm(x, g, tm):
    m, d = x.shape
    return pl.pallas_call(
        _rmsnorm_kernel,
        grid=(m // tm,),
        in_specs=[pl.BlockSpec((tm, d), lambda i: (i, 0)), pl.BlockSpec((1, d), lambda i: (0, 0))],
        out_specs=pl.BlockSpec((tm, d), lambda i: (i, 0)),
        out_shape=jax.ShapeDtypeStruct((m, d), BF16),
        compiler_params=_params("parallel"),
        name="rmsnorm",
    )(x, g.reshape(1, d))


def _mm_kernel(x_ref, w_ref, *refs, nk, n_extra, n_out, epilogue):
    extras, outs = refs[:n_extra], refs[n_extra:n_extra + n_out]
    if nk == 1:
        acc = jnp.dot(x_ref[...], w_ref[...], preferred_element_type=F32)
        epilogue(acc, *extras, *outs)
        return
    acc_ref = refs[n_extra + n_out]
    k = pl.program_id(2)

    @pl.when(k == 0)
    def _():
        acc_ref[...] = jnp.zeros_like(acc_ref)

    acc_ref[...] += jnp.dot(x_ref[...], w_ref[...], preferred_element_type=F32)

    @pl.when(k == nk - 1)
    def _():
        epilogue(acc_ref[...], *extras, *outs)


def _matmul(x, w, col0, ncols, tm, tn, epilogue, extras, outs, name):
    m, kdim = x.shape
    tk = min(kdim, MAX_TK)
    nk = kdim // tk
    assert col0 % tn == 0 and ncols % tn == 0 and m % tm == 0 and kdim % tk == 0
    cb = col0 // tn
    in_specs = [pl.BlockSpec((tm, tk), lambda n, i, k: (i, k)),
                pl.BlockSpec((tk, tn), lambda n, i, k: (k, n + cb))]
    in_specs += [pl.BlockSpec(bs, im) for _, bs, im in extras]
    out_specs = [pl.BlockSpec((tm, tn), lambda n, i, k: (i, n)) for _ in outs]
    out_shape = [jax.ShapeDtypeStruct((m, ncols), dt) for dt in outs]
    scratch = [pltpu.VMEM((tm, tn), F32)] if nk > 1 else []
    res = pl.pallas_call(
        functools.partial(_mm_kernel, nk=nk, n_extra=len(extras), n_out=len(outs), epilogue=epilogue),
        grid=(ncols // tn, m // tm, nk),
        in_specs=in_specs,
        out_specs=out_specs,
        out_shape=out_shape,
        scratch_shapes=scratch,
        compiler_params=_params("parallel", "parallel", "arbitrary"),
        name=name,
    )(x, w, *[a for a, _, _ in extras])
    return res


def _epi_plain(acc, *outs):
    for o in outs:
        o[...] = acc.astype(o.dtype)


def _epi_norm_rope(acc, g_ref, cos_ref, sin_ref, *outs, scale):
    g = g_ref[...]
    cos = cos_ref[...]
    sin = sin_ref[...]
    for j in range(acc.shape[1] // HEAD_DIM):
        z = acc[:, j * HEAD_DIM:(j + 1) * HEAD_DIM]
        ms = jnp.mean(z * z, axis=-1, keepdims=True)
        y = z * lax.rsqrt(ms + EPS) * g
        y = y * cos + pltpu.roll(y, HEAD_DIM // 2, axis=1) * sin
        if scale != 1.0:
            y = y * scale
        for o in outs:
            o[:, j * HEAD_DIM:(j + 1) * HEAD_DIM] = y.astype(o.dtype)


def _epi_gate(acc, b_ref, o_ref):
    o_ref[...] = jax.nn.sigmoid(acc + b_ref[...]).astype(o_ref.dtype)


def _epi_residual(acc, r_ref, o_ref):
    o_ref[...] = (r_ref[...] + acc).astype(o_ref.dtype)


def _epi_relu2(acc, o_ref):
    o_ref[...] = jnp.square(jnp.maximum(acc, 0.0)).astype(o_ref.dtype)


def _rope_tables(pos):
    half = HEAD_DIM // 2
    inv = ROPE_THETA ** (-jnp.arange(half, dtype=F32) / half)
    ang = pos.astype(F32)[:, None] * inv[None, :]
    cos, sin = jnp.cos(ang), jnp.sin(ang)
    return jnp.concatenate([cos, cos], axis=-1), jnp.concatenate([-sin, sin], axis=-1)


def _lambda(lq1, lk1, lq2, lk2, lambda_init):
    s1 = jnp.sum(lq1[...] * lk1[...], axis=-1, keepdims=True)
    s2 = jnp.sum(lq2[...] * lk2[...], axis=-1, keepdims=True)
    return jnp.exp(s1) - jnp.exp(s2) + lambda_init


def _subln(o, subln_ref, lambda_init):
    ms = jnp.mean(o * o, axis=-1, keepdims=True)
    return o * lax.rsqrt(ms + EPS) * subln_ref[...] * (1.0 - lambda_init)


def _attn_prompt_kernel(lq1, lk1, lq2, lk2, subln_ref, q_ref, k_ref, v_ref, o_ref,
                        acc_ref, m_ref, l_ref, *, tq, lambda_init):
    qi = pl.program_id(2)
    d = HEAD_DIM
    m_ref[...] = jnp.full_like(m_ref, NEG_INF)
    l_ref[...] = jnp.zeros_like(l_ref)
    acc_ref[...] = jnp.zeros_like(acc_ref)

    def step(j, masked):
        off = pl.multiple_of(j * tq, tq)
        kb = k_ref[pl.ds(off, tq), :]
        vb = v_ref[pl.ds(off, tq), :]
        for c in range(2):
            qc = q_ref[:, c * d:(c + 1) * d]
            s = lax.dot_general(qc, kb[:, c * d:(c + 1) * d], (((1,), (1,)), ((), ())),
                                preferred_element_type=F32)
            if masked:
                qpos = qi * tq + lax.broadcasted_iota(jnp.int32, s.shape, 0)
                kpos = j * tq + lax.broadcasted_iota(jnp.int32, s.shape, 1)
                s = jnp.where(kpos // CHUNK <= qpos // CHUNK, s, NEG_INF)
            m_prev = m_ref[c]
            m_new = jnp.maximum(m_prev, jnp.max(s, axis=-1, keepdims=True))
            alpha = jnp.exp(m_prev - m_new)
            p = jnp.exp(s - m_new)
            l_ref[c] = alpha * l_ref[c] + jnp.sum(p, axis=-1, keepdims=True)
            acc_ref[c] = alpha * acc_ref[c] + jnp.dot(p.astype(BF16), vb, preferred_element_type=F32)
            m_ref[c] = m_new

    def body(j, carry):
        step(j, False)
        return carry

    lax.fori_loop(0, qi, body, 0)
    step(qi, True)

    lam = _lambda(lq1, lk1, lq2, lk2, lambda_init)
    o = acc_ref[0] / l_ref[0] - lam * (acc_ref[1] / l_ref[1])
    o_ref[...] = _subln(o, subln_ref, lambda_init).astype(o_ref.dtype)


def _attn_prompt(q, k, v, lam_vecs, subln, lambda_init, batch, tq):
    mt, width = q.shape
    t = mt // batch
    hd2 = 2 * HEAD_DIM
    nh = width // hd2
    nq = t // tq
    vec = pl.BlockSpec((1, HEAD_DIM), lambda b, h, i: (0, 0))
    return pl.pallas_call(
        functools.partial(_attn_prompt_kernel, tq=tq, lambda_init=lambda_init),
        grid=(batch, nh, nq),
        in_specs=[vec, vec, vec, vec,
                  pl.BlockSpec((1, hd2), lambda b, h, i: (0, 0)),
                  pl.BlockSpec((tq, hd2), lambda b, h, i: (b * nq + i, h)),
                  pl.BlockSpec((t, hd2), lambda b, h, i: (b, h)),
                  pl.BlockSpec((t, hd2), lambda b, h, i: (b, h))],
        out_specs=pl.BlockSpec((tq, hd2), lambda b, h, i: (b * nq + i, h)),
        out_shape=jax.ShapeDtypeStruct((mt, width), F32),
        scratch_shapes=[pltpu.VMEM((2, tq, hd2), F32), pltpu.VMEM((2, tq, 1), F32),
                        pltpu.VMEM((2, tq, 1), F32)],
        compiler_params=_params("parallel", "parallel", "arbitrary"),
        name="attn_prompt",
    )(*lam_vecs, subln, q, k, v)


def _attn_sample_kernel(lq1, lk1, lq2, lk2, subln_ref, q_ref, kn_ref, vn_ref, kc_ref, vc_ref, o_ref,
                        *, past_len, lambda_init):
    d = HEAD_DIM
    q = q_ref[...]
    tq = q.shape[0]
    z = jnp.zeros((tq, d), q.dtype)
    qs = jnp.concatenate([jnp.concatenate([q[:, :d], z], axis=1),
                          jnp.concatenate([z, q[:, d:]], axis=1)], axis=0)
    nt = (((1,), (1,)), ((), ()))
    pad = jnp.zeros((LANES - tq, 2 * d), q.dtype)
    kn = jnp.concatenate([kn_ref[...], pad], axis=0)
    vn = jnp.concatenate([vn_ref[...], pad], axis=0)
    s_p = lax.dot_general(qs, kc_ref[...].astype(BF16), nt, preferred_element_type=F32)
    s_n = lax.dot_general(qs, kn, nt, preferred_element_type=F32)

    def qchunk(shape):
        return (past_len + lax.broadcasted_iota(jnp.int32, shape, 0) % tq) // CHUNK

    s_p = jnp.where(lax.broadcasted_iota(jnp.int32, s_p.shape, 1) // CHUNK <= qchunk(s_p.shape), s_p, NEG_INF)
    col_n = lax.broadcasted_iota(jnp.int32, s_n.shape, 1)
    vis_n = (col_n < tq) & ((past_len + col_n) // CHUNK <= qchunk(s_n.shape))
    s_n = jnp.where(vis_n, s_n, NEG_INF)
    m = jnp.maximum(jnp.max(s_p, axis=-1, keepdims=True), jnp.max(s_n, axis=-1, keepdims=True))
    p_p = jnp.exp(s_p - m)
    p_n = jnp.where(col_n < tq, jnp.exp(s_n - m), 0.0)
    l = jnp.sum(p_p, axis=-1, keepdims=True) + jnp.sum(p_n, axis=-1, keepdims=True)
    o2 = (jnp.dot(p_p.astype(BF16), vc_ref[...].astype(BF16), preferred_element_type=F32)
          + jnp.dot(p_n.astype(BF16), vn, preferred_element_type=F32)) / l
    lam = _lambda(lq1, lk1, lq2, lk2, lambda_init)
    o = o2[:tq] - lam * o2[tq:]
    o_ref[...] = _subln(o, subln_ref, lambda_init).astype(o_ref.dtype)


def _attn_sample(q, kn, vn, cache_k, cache_v, lam_vecs, subln, lambda_init, tq):
    mt, width = q.shape
    nb, past_len, _ = cache_k.shape
    hd2 = 2 * HEAD_DIM
    nh = width // hd2
    vec = pl.BlockSpec((1, HEAD_DIM), lambda b, h: (0, 0))
    row = pl.BlockSpec((tq, hd2), lambda b, h: (b, h))
    cache = pl.BlockSpec((None, past_len, hd2), lambda b, h: (b, 0, h))
    return pl.pallas_call(
        functools.partial(_attn_sample_kernel, past_len=past_len, lambda_init=lambda_init),
        grid=(nb, nh),
        in_specs=[vec, vec, vec, vec, pl.BlockSpec((1, hd2), lambda b, h: (0, 0)),
                  row, row, row, cache, cache],
        out_specs=row,
        out_shape=jax.ShapeDtypeStruct((mt, width), F32),
        compiler_params=_params("parallel", "parallel"),
        name="attn_sample",
    )(*lam_vecs, subln, q, kn, vn, cache_k, cache_v)


def _sgu_merge_kernel(vs_ref, u_ref, ga_ref, gb_ref, oa_ref, gn_ref, w_ref, bt_ref, *outs,
                      seg, groups, want_v):
    vs = vs_ref[...]
    ms = jnp.mean(vs * vs, axis=-1, keepdims=True)
    vn = vs * lax.rsqrt(ms + EPS) * gn_ref[...]
    if want_v:
        outs[1][...] = vn
    rows = vs.shape[0]
    gd = vs.shape[1] // groups
    r = lax.broadcasted_iota(jnp.int32, (rows, rows), 0)
    c = lax.broadcasted_iota(jnp.int32, (rows, rows), 1)
    keep = (r // seg == c // seg) & (c <= r)
    vb = vn.astype(BF16)
    for g in range(groups):
        sl = slice(g * gd, (g + 1) * gd)
        w = jnp.where(keep, w_ref[g], 0.0).astype(BF16)
        s = jnp.dot(w, vb[:, sl], preferred_element_type=F32) + bt_ref[:, g:g + 1]
        o_sgu = u_ref[:, sl] * s
        outs[0][:, sl] = (ga_ref[:, sl] * oa_ref[:, sl] + gb_ref[:, sl] * o_sgu).astype(outs[0].dtype)


def _sgu_merge(vs, u, gate, o_attn, sgu_norm, w_s, b_s, seg, want_v):
    m, d = vs.shape
    rows = SGU_CHUNK
    groups = w_s.shape[0]
    rep = rows // seg
    w_t = jnp.tile(w_s[:, :seg, :seg], (1, rep, rep))
    b_t = jnp.tile(jnp.swapaxes(b_s[:, :seg], 0, 1), (rep, 1))
    blk = pl.BlockSpec((rows, d), lambda i: (i, 0))
    out_shape = [jax.ShapeDtypeStruct((m, d), BF16)]
    if want_v:
        out_shape.append(jax.ShapeDtypeStruct((m, d), F32))
    return pl.pallas_call(
        functools.partial(_sgu_merge_kernel, seg=seg, groups=groups, want_v=want_v),
        grid=(m // rows,),
        in_specs=[blk, blk, blk, pl.BlockSpec((rows, d), lambda i: (i, 1)), blk,
                  pl.BlockSpec((1, d), lambda i: (0, 0)),
                  pl.BlockSpec((groups, rows, rows), lambda i: (0, 0, 0)),
                  pl.BlockSpec((rows, groups), lambda i: (0, 0))],
        out_specs=[blk] * len(out_shape),
        out_shape=out_shape,
        compiler_params=_params("parallel"),
        name="sgu_merge",
    )(vs, u, gate, gate, o_attn, sgu_norm.reshape(1, d), w_t, b_t)


def _layer(x, pos, attend, seg, want_v, tm, lambda_init, wts):
    (norm_mix, w_in, b_gate, q_norm, k_norm, subln, sgu_norm, w_s, b_s, w_o, norm_ffn, w_up, w_down) = wts
    m, d = x.shape
    qk_w = N_HEADS * 2 * HEAD_DIM
    tn = min(1024, d)
    cos, sin = _rope_tables(pos)
    npos = pos.shape[0] // tm
    rope_extras = lambda g: [(g.reshape(1, HEAD_DIM), (1, HEAD_DIM), lambda n, i, k: (0, 0)),
                             (cos, (tm, HEAD_DIM), lambda n, i, k: (i % npos, 0)),
                             (sin, (tm, HEAD_DIM), lambda n, i, k: (i % npos, 0))]

    h = _rmsnorm(x, norm_mix, min(tm, 256))
    (q,) = _matmul(h, w_in, 0, qk_w, tm, tn, functools.partial(_epi_norm_rope, scale=HEAD_DIM ** -0.5),
                   rope_extras(q_norm), [BF16], "proj_q")
    k32, kb = _matmul(h, w_in, qk_w, qk_w, tm, tn, functools.partial(_epi_norm_rope, scale=1.0),
                      rope_extras(k_norm), [F32, BF16], "proj_k")
    v32, vb = _matmul(h, w_in, 2 * qk_w, qk_w, tm, tn, _epi_plain, [], [F32, BF16], "proj_v")
    (u,) = _matmul(h, w_in, 3 * qk_w, d, tm, tn, _epi_plain, [], [F32], "proj_u")
    (vs,) = _matmul(h, w_in, 3 * qk_w + d, d, tm, tn, _epi_plain, [], [F32], "proj_vs")
    (gate,) = _matmul(h, w_in, 3 * qk_w + 2 * d, 2 * d, tm, tn, _epi_gate,
                      [(b_gate.reshape(1, 2 * d), (1, tn), lambda n, i, k: (0, n))], [F32], "proj_gate")

    o_attn = attend(q, kb, vb)
    res = _sgu_merge(vs, u, gate, o_attn, sgu_norm, w_s, b_s, seg, want_v)
    merged = res[0]
    res_spec = lambda a: [(a, (tm, tn), lambda n, i, k: (i, n))]
    (x1,) = _matmul(merged, w_o, 0, d, tm, tn, _epi_residual, res_spec(x), [F32], "proj_out")
    hf = _rmsnorm(x1, norm_ffn, min(tm, 256))
    (hid,) = _matmul(hf, w_up, 0, w_up.shape[1], tm, tn, _epi_relu2, [], [BF16], "ffn_up")
    (y,) = _matmul(hid, w_down, 0, d, tm, tn, _epi_residual, res_spec(x1), [F32], "ffn_down")
    return y, k32, v32, (res[1] if want_v else None)


def kernel(x_prompt, x_sample, cache_k_attn, cache_v_attn, norm_mix, w_in, b_gate, q_norm, k_norm, lambda_q1, lambda_k1, lambda_q2, lambda_k2, subln, sgu_norm, w_s, b_s, w_o, norm_ffn, w_up, w_down):
    depth = w_in.shape[0]
    bp, tp, d = x_prompt.shape
    bs, ts, _ = x_sample.shape
    hd2 = 2 * HEAD_DIM
    pos_p = jnp.arange(tp, dtype=jnp.int32)
    pos_s = jnp.tile(PAST_LEN + jnp.arange(ts, dtype=jnp.int32), bs)
    xp = x_prompt.reshape(bp * tp, d)
    xs = x_sample.reshape(bs * ts, d)
    kp_l, vp_l, ks_l, vs_l, sg_l = [], [], [], [], []
    for l in range(depth):
        lambda_init = 0.8 - 0.6 * math.exp(-0.3 * l)
        lam_vecs = [a[l].reshape(1, HEAD_DIM) for a in (lambda_q1, lambda_k1, lambda_q2, lambda_k2)]
        sub = subln[l].reshape(1, hd2)
        wts = (norm_mix[l], w_in[l].astype(BF16), b_gate[l], q_norm[l], k_norm[l], sub, sgu_norm[l],
               w_s[l], b_s[l], w_o[l].astype(BF16), norm_ffn[l], w_up[l].astype(BF16), w_down[l].astype(BF16))
        ck = cache_k_attn[l].reshape(bs, PAST_LEN, N_HEADS * hd2)
        cv = cache_v_attn[l].reshape(bs, PAST_LEN, N_HEADS * hd2)

        attend_p = lambda q, k, v: _attn_prompt(q, k, v, lam_vecs, sub, lambda_init, bp, min(512, tp))
        attend_s = lambda q, k, v: _attn_sample(q, k, v, ck, cv, lam_vecs, sub, lambda_init, ts)
        xp, kp, vp, _ = _layer(xp, pos_p, attend_p, min(tp, SGU_CHUNK), False, 512, lambda_init, wts)
        xs, ksm, vsm, sgv = _layer(xs, pos_s, attend_s, min(ts, SGU_CHUNK), True, bs * ts, lambda_init, wts)
        kp_l.append(kp.reshape(bp, tp, N_HEADS, hd2))
        vp_l.append(vp.reshape(bp, tp, N_HEADS, hd2))
        ks_l.append(ksm.reshape(bs, ts, N_HEADS, hd2))
        vs_l.append(vsm.reshape(bs, ts, N_HEADS, hd2))
        sg_l.append(sgv.reshape(bs, ts, d))
    return (xp.reshape(bp, tp, d), xs.reshape(bs, ts, d), jnp.stack(kp_l), jnp.stack(vp_l),
            jnp.stack(ks_l), jnp.stack(vs_l), jnp.stack(sg_l))
```

```python
import functools
import math

import jax
import jax.numpy as jnp
from jax import lax
from jax.experimental import pallas as pl
from jax.experimental.pallas import tpu as pltpu

D_MODEL = 4096
BATCH = 2
SEQ = 4096
DEC_BATCH = 16
DEC_SEQ = 16
PAST_LEN = 4096
CHUNK = 64
N_HEADS = 16
HEAD_DIM = 128
SGU_CHUNK = 128
SGU_GROUPS = 8
D_FF = 4 * D_MODEL
ROPE_THETA = 10000.0
EPS = 1e-6
NEG_INF = -1e30

LANES = 128
VMEM_LIMIT_BYTES = 56 * 1024 * 1024
MAX_TK = 4096
SAMPLE_CACHE_BLOCK = 512
LOG2E = math.log2(math.e)
F32 = jnp.float32
BF16 = jnp.bfloat16


def _params(*sem):
    return pltpu.CompilerParams(dimension_semantics=sem, vmem_limit_bytes=VMEM_LIMIT_BYTES)


def _rmsnorm_kernel(x_ref, g_ref, o_ref):
    x = x_ref[...]
    ms = jnp.mean(x * x, axis=-1, keepdims=True)
    o_ref[...] = (x * lax.rsqrt(ms + EPS) * g_ref[...]).astype(o_ref.dtype)


def _rmsnorm(x, g, tm):
    m, d = x.shape
    return pl.pallas_call(
        _rmsnorm_kernel,
        grid=(m // tm,),
        in_specs=[pl.BlockSpec((tm, d), lambda i: (i, 0)), pl.BlockSpec((1, d), lambda i: (0, 0))],
        out_specs=pl.BlockSpec((tm, d), lambda i: (i, 0)),
        out_shape=jax.ShapeDtypeStruct((m, d), BF16),
        compiler_params=_params("parallel"),
        name="rmsnorm",
    )(x, g.reshape(1, d))


def _mm_kernel(x_ref, w_ref, *refs, nk, n_extra, n_out, epilogue):
    extras, outs = refs[:n_extra], refs[n_extra:n_extra + n_out]
    if nk == 1:
        acc = jnp.dot(x_ref[...], w_ref[...], preferred_element_type=F32)
        epilogue(acc, *extras, *outs)
        return
    acc_ref = refs[n_extra + n_out]
    k = pl.program_id(2)

    @pl.when(k == 0)
    def _():
        acc_ref[...] = jnp.zeros_like(acc_ref)

    acc_ref[...] += jnp.dot(x_ref[...], w_ref[...], preferred_element_type=F32)

    @pl.when(k == nk - 1)
    def _():
        epilogue(acc_ref[...], *extras, *outs)


def _matmul(x, w, col0, ncols, tm, tn, epilogue, extras, outs, name):
    m, kdim = x.shape
    tk = min(kdim, MAX_TK)
    nk = kdim // tk
    assert col0 % tn == 0 and ncols % tn == 0 and m % tm == 0 and kdim % tk == 0
    cb = col0 // tn
    in_specs = [pl.BlockSpec((tm, tk), lambda n, i, k: (i, k)),
                pl.BlockSpec((tk, tn), lambda n, i, k: (k, n + cb))]
    in_specs += [pl.BlockSpec(bs, im) for _, bs, im in extras]
    out_specs = [pl.BlockSpec((tm, tn), lambda n, i, k: (i, n)) for _ in outs]
    out_shape = [jax.ShapeDtypeStruct((m, ncols), dt) for dt in outs]
    scratch = [pltpu.VMEM((tm, tn), F32)] if nk > 1 else []
    res = pl.pallas_call(
        functools.partial(_mm_kernel, nk=nk, n_extra=len(extras), n_out=len(outs), epilogue=epilogue),
        grid=(ncols // tn, m // tm, nk),
        in_specs=in_specs,
        out_specs=out_specs,
        out_shape=out_shape,
        scratch_shapes=scratch,
        compiler_params=_params("parallel", "parallel", "arbitrary"),
        name=name,
    )(x, w, *[a for a, _, _ in extras])
    return res


def _epi_plain(acc, *outs):
    for o in outs:
        o[...] = acc.astype(o.dtype)


def _epi_norm_rope(acc, g_ref, cos_ref, sin_ref, *outs, scale):
    g = g_ref[...]
    cos = cos_ref[...]
    sin = sin_ref[...]
    for j in range(acc.shape[1] // HEAD_DIM):
        z = acc[:, j * HEAD_DIM:(j + 1) * HEAD_DIM]
        ms = jnp.mean(z * z, axis=-1, keepdims=True)
        y = z * lax.rsqrt(ms + EPS) * g
        y = y * cos + pltpu.roll(y, HEAD_DIM // 2, axis=1) * sin
        if scale != 1.0:
            y = y * scale
        for o in outs:
            o[:, j * HEAD_DIM:(j + 1) * HEAD_DIM] = y.astype(o.dtype)


def _epi_gate(acc, b_ref, o_ref):
    o_ref[...] = jax.nn.sigmoid(acc + b_ref[...]).astype(o_ref.dtype)


def _epi_residual(acc, r_ref, o_ref):
    o_ref[...] = (r_ref[...] + acc).astype(o_ref.dtype)


def _epi_relu2(acc, o_ref):
    o_ref[...] = jnp.square(jnp.maximum(acc, 0.0)).astype(o_ref.dtype)


def _rope_tables(pos):
    half = HEAD_DIM // 2
    inv = ROPE_THETA ** (-jnp.arange(half, dtype=F32) / half)
    ang = pos.astype(F32)[:, None] * inv[None, :]
    cos, sin = jnp.cos(ang), jnp.sin(ang)
    return jnp.concatenate([cos, cos], axis=-1), jnp.concatenate([-sin, sin], axis=-1)


def _lambda(lq1, lk1, lq2, lk2, lambda_init):
    s1 = jnp.sum(lq1[...] * lk1[...], axis=-1, keepdims=True)
    s2 = jnp.sum(lq2[...] * lk2[...], axis=-1, keepdims=True)
    return jnp.exp(s1) - jnp.exp(s2) + lambda_init


def _subln(o, subln_ref, lambda_init):
    ms = jnp.mean(o * o, axis=-1, keepdims=True)
    return o * lax.rsqrt(ms + EPS) * subln_ref[...] * (1.0 - lambda_init)


def _attn_prompt_kernel(lq1, lk1, lq2, lk2, subln_ref, bias_ref, q_ref, k_ref, v_ref, o_ref,
                        s_ref, acc_ref, mp_ref, mb_ref, lp_ref, *, tq, lambda_init):
    qi = pl.program_id(2)
    d = HEAD_DIM
    nl = tq // LANES
    nt = (((1,), (1,)), ((), ()))
    mp_ref[...] = jnp.full_like(mp_ref, NEG_INF)

    def scores(j, masked):
        off = pl.multiple_of(j * tq, tq)
        for c in range(2):
            s = lax.dot_general(q_ref[:, c * d:(c + 1) * d], k_ref[pl.ds(off, tq), c * d:(c + 1) * d], nt,
                                preferred_element_type=F32)
            if masked:
                s = s + bias_ref[...]
            s_ref[c, j] = s
            m = mp_ref[c]
            for t in range(nl):
                m = jnp.maximum(m, s[:, t * LANES:(t + 1) * LANES])
            mp_ref[c] = m

    def scores_body(j, carry):
        scores(j, False)
        return carry

    lax.fori_loop(0, qi, scores_body, 0)
    scores(qi, True)

    for c in range(2):
        mb_ref[c] = jnp.broadcast_to(jnp.max(mp_ref[c], axis=-1, keepdims=True), (tq, LANES))
    lp_ref[...] = jnp.zeros_like(lp_ref)
    acc_ref[...] = jnp.zeros_like(acc_ref)

    def accumulate(j, carry):
        off = pl.multiple_of(j * tq, tq)
        vb = v_ref[pl.ds(off, tq), :]
        for c in range(2):
            s = s_ref[c, j]
            mb = mb_ref[c]
            ps = [jnp.exp2(s[:, t * LANES:(t + 1) * LANES] - mb) for t in range(nl)]
            lp = lp_ref[c]
            for p in ps:
                lp = lp + p
            lp_ref[c] = lp
            pb = jnp.concatenate(ps, axis=1).astype(BF16)
            acc_ref[c] += jnp.dot(pb, vb, preferred_element_type=F32)
        return carry

    lax.fori_loop(0, qi + 1, accumulate, 0)

    lam = _lambda(lq1, lk1, lq2, lk2, lambda_init)
    l0 = jnp.sum(lp_ref[0], axis=-1, keepdims=True)
    l1 = jnp.sum(lp_ref[1], axis=-1, keepdims=True)
    o = acc_ref[0] / l0 - lam * (acc_ref[1] / l1)
    o_ref[...] = _subln(o, subln_ref, lambda_init).astype(o_ref.dtype)


def _attn_prompt(q, k, v, lam_vecs, subln, lambda_init, batch, tq):
    mt, width = q.shape
    t = mt // batch
    hd2 = 2 * HEAD_DIM
    nh = width // hd2
    nq = t // tq
    chunk_of = jnp.arange(tq, dtype=jnp.int32) // CHUNK
    bias = jnp.where(chunk_of[None, :] <= chunk_of[:, None], 0.0, NEG_INF).astype(F32)
    vec = pl.BlockSpec((1, HEAD_DIM), lambda b, h, i: (0, 0))
    return pl.pallas_call(
        functools.partial(_attn_prompt_kernel, tq=tq, lambda_init=lambda_init),
        grid=(batch, nh, nq),
        in_specs=[vec, vec, vec, vec,
                  pl.BlockSpec((1, hd2), lambda b, h, i: (0, 0)),
                  pl.BlockSpec((tq, tq), lambda b, h, i: (0, 0)),
                  pl.BlockSpec((tq, hd2), lambda b, h, i: (b * nq + i, h)),
                  pl.BlockSpec((t, hd2), lambda b, h, i: (b, h)),
                  pl.BlockSpec((t, hd2), lambda b, h, i: (b, h))],
        out_specs=pl.BlockSpec((tq, hd2), lambda b, h, i: (b * nq + i, h)),
        out_shape=jax.ShapeDtypeStruct((mt, width), F32),
        scratch_shapes=[pltpu.VMEM((2, nq, tq, tq), F32), pltpu.VMEM((2, tq, hd2), F32),
                        pltpu.VMEM((2, tq, LANES), F32), pltpu.VMEM((2, tq, LANES), F32),
                        pltpu.VMEM((2, tq, LANES), F32)],
        compiler_params=_params("parallel", "parallel", "arbitrary"),
        name="attn_prompt",
    )(*lam_vecs, subln, bias, q, k, v)


def _attn_sample_kernel(lq1, lk1, lq2, lk2, subln_ref, q_ref, kn_ref, vn_ref, k0_ref, k1_ref, v0_ref, v1_ref,
                        o_ref, m_ref, l_ref, acc_ref, *, past_len, nheads, lambda_init):
    d = HEAD_DIM
    tq = q_ref.shape[0]
    tp = k0_ref.shape[0] // nheads
    pc = pl.program_id(1)
    nt = (((1,), (1,)), ((), ()))

    @pl.when(pc == 0)
    def _():
        m_ref[...] = jnp.full_like(m_ref, NEG_INF)
        l_ref[...] = jnp.zeros_like(l_ref)
        acc_ref[...] = jnp.zeros_like(acc_ref)

    def stacked_q(h):
        q = q_ref[:, h * 2 * d:(h + 1) * 2 * d]
        z = jnp.zeros((tq, d), q.dtype)
        return jnp.concatenate([jnp.concatenate([q[:, :d], z], axis=1),
                                jnp.concatenate([z, q[:, d:]], axis=1)], axis=0)

    def qchunk(shape):
        return (past_len + lax.broadcasted_iota(jnp.int32, shape, 0) % tq) // CHUNK

    def update(h, s, v, valid):
        m_prev = m_ref[h]
        m_new = jnp.maximum(m_prev, jnp.max(s, axis=-1, keepdims=True))
        alpha = jnp.exp2(m_prev - m_new)
        p = jnp.exp2(s - m_new)
        if valid is not None:
            p = jnp.where(valid, p, 0.0)
        l_new = alpha * l_ref[h] + jnp.sum(p, axis=-1, keepdims=True)
        acc_new = alpha * acc_ref[h] + jnp.dot(p.astype(BF16), v, preferred_element_type=F32)
        return m_new, l_new, acc_new

    vis = (pc * tp + lax.broadcasted_iota(jnp.int32, (2 * tq, tp), 1)) // CHUNK <= qchunk((2 * tq, tp))
    for h in range(nheads):
        rows_h = pl.ds(h, tp, stride=nheads)
        kb = jnp.concatenate([k0_ref[rows_h, :], k1_ref[rows_h, :]], axis=1).astype(BF16)
        vb = jnp.concatenate([v0_ref[rows_h, :], v1_ref[rows_h, :]], axis=1).astype(BF16)
        s = lax.dot_general(stacked_q(h), kb, nt, preferred_element_type=F32)
        m_new, l_new, acc_new = update(h, jnp.where(vis, s, NEG_INF), vb, None)
        m_ref[h] = m_new
        l_ref[h] = l_new
        acc_ref[h] = acc_new

    @pl.when(pc == pl.num_programs(1) - 1)
    def _():
        lam = _lambda(lq1, lk1, lq2, lk2, lambda_init)
        pad = jnp.zeros((LANES - tq, 2 * d), kn_ref.dtype)
        col = lax.broadcasted_iota(jnp.int32, (2 * tq, LANES), 1)
        valid = col < tq
        vis_n = valid & ((past_len + col) // CHUNK <= qchunk((2 * tq, LANES)))
        for h in range(nheads):
            sl = slice(h * 2 * d, (h + 1) * 2 * d)
            kn = jnp.concatenate([kn_ref[:, sl], pad], axis=0)
            vn = jnp.concatenate([vn_ref[:, sl], pad], axis=0)
            s = lax.dot_general(stacked_q(h), kn, nt, preferred_element_type=F32)
            _, l_new, acc_new = update(h, jnp.where(vis_n, s, NEG_INF), vn, valid)
            o2 = acc_new / l_new
            o = o2[:tq] - lam * o2[tq:]
            o_ref[:, sl] = _subln(o, subln_ref, lambda_init).astype(o_ref.dtype)


def _attn_sample(q, kn, vn, cache_k, cache_v, lam_vecs, subln, lambda_init, tq):
    mt, width = q.shape
    nb, past_len, nh, hd2 = cache_k.shape
    tp = min(SAMPLE_CACHE_BLOCK, past_len)
    cache_k = cache_k.reshape(nb, past_len * nh, hd2)
    cache_v = cache_v.reshape(nb, past_len * nh, hd2)
    vec = pl.BlockSpec((1, HEAD_DIM), lambda b, p: (0, 0))
    row = pl.BlockSpec((tq, width), lambda b, p: (b, 0))
    half = [pl.BlockSpec((None, tp * nh, HEAD_DIM), functools.partial(lambda b, p, c: (b, p, c), c=c))
            for c in range(2)]
    return pl.pallas_call(
        functools.partial(_attn_sample_kernel, past_len=past_len, nheads=nh, lambda_init=lambda_init),
        grid=(nb, past_len // tp),
        in_specs=[vec, vec, vec, vec, pl.BlockSpec((1, hd2), lambda b, p: (0, 0)),
                  row, row, row, half[0], half[1], half[0], half[1]],
        out_specs=row,
        out_shape=jax.ShapeDtypeStruct((mt, width), F32),
        scratch_shapes=[pltpu.VMEM((nh, 2 * tq, 1), F32), pltpu.VMEM((nh, 2 * tq, 1), F32),
                        pltpu.VMEM((nh, 2 * tq, hd2), F32)],
        compiler_params=_params("parallel", "arbitrary"),
        name="attn_sample",
    )(*lam_vecs, subln, q, kn, vn, cache_k, cache_k, cache_v, cache_v)


def _sgu_merge_kernel(vs_ref, u_ref, ga_ref, gb_ref, oa_ref, gn_ref, w_ref, bt_ref, *outs,
                      seg, groups, want_v):
    vs = vs_ref[...]
    ms = jnp.mean(vs * vs, axis=-1, keepdims=True)
    vn = vs * lax.rsqrt(ms + EPS) * gn_ref[...]
    if want_v:
        outs[1][...] = vn
    rows = vs.shape[0]
    gd = vs.shape[1] // groups
    r = lax.broadcasted_iota(jnp.int32, (rows, rows), 0)
    c = lax.broadcasted_iota(jnp.int32, (rows, rows), 1)
    keep = (r // seg == c // seg) & (c <= r)
    vb = vn.astype(BF16)
    for g in range(groups):
        sl = slice(g * gd, (g + 1) * gd)
        w = jnp.where(keep, w_ref[g], 0.0).astype(BF16)
        s = jnp.dot(w, vb[:, sl], preferred_element_type=F32) + bt_ref[:, g:g + 1]
        o_sgu = u_ref[:, sl] * s
        outs[0][:, sl] = (ga_ref[:, sl] * oa_ref[:, sl] + gb_ref[:, sl] * o_sgu).astype(outs[0].dtype)


def _sgu_merge(vs, u, gate, o_attn, sgu_norm, w_s, b_s, seg, want_v):
    m, d = vs.shape
    rows = SGU_CHUNK
    groups = w_s.shape[0]
    rep = rows // seg
    w_t = jnp.tile(w_s[:, :seg, :seg], (1, rep, rep))
    b_t = jnp.tile(jnp.swapaxes(b_s[:, :seg], 0, 1), (rep, 1))
    blk = pl.BlockSpec((rows, d), lambda i: (i, 0))
    out_shape = [jax.ShapeDtypeStruct((m, d), BF16)]
    if want_v:
        out_shape.append(jax.ShapeDtypeStruct((m, d), F32))
    return pl.pallas_call(
        functools.partial(_sgu_merge_kernel, seg=seg, groups=groups, want_v=want_v),
        grid=(m // rows,),
        in_specs=[blk, blk, blk, pl.BlockSpec((rows, d), lambda i: (i, 1)), blk,
                  pl.BlockSpec((1, d), lambda i: (0, 0)),
                  pl.BlockSpec((groups, rows, rows), lambda i: (0, 0, 0)),
                  pl.BlockSpec((rows, groups), lambda i: (0, 0))],
        out_specs=[blk] * len(out_shape),
        out_shape=out_shape,
        compiler_params=_params("parallel"),
        name="sgu_merge",
    )(vs, u, gate, gate, o_attn, sgu_norm.reshape(1, d), w_t, b_t)


def _layer(x, pos, attend, seg, want_v, tm, lambda_init, wts):
    (norm_mix, w_in, b_gate, q_norm, k_norm, subln, sgu_norm, w_s, b_s, w_o, norm_ffn, w_up, w_down) = wts
    m, d = x.shape
    qk_w = N_HEADS * 2 * HEAD_DIM
    tn = min(1024, d)
    cos, sin = _rope_tables(pos)
    npos = pos.shape[0] // tm
    rope_extras = lambda g: [(g.reshape(1, HEAD_DIM), (1, HEAD_DIM), lambda n, i, k: (0, 0)),
                             (cos, (tm, HEAD_DIM), lambda n, i, k: (i % npos, 0)),
                             (sin, (tm, HEAD_DIM), lambda n, i, k: (i % npos, 0))]

    h = _rmsnorm(x, norm_mix, min(tm, 256))
    (q,) = _matmul(h, w_in, 0, qk_w, tm, tn, functools.partial(_epi_norm_rope, scale=HEAD_DIM ** -0.5 * LOG2E),
                   rope_extras(q_norm), [BF16], "proj_q")
    k32, kb = _matmul(h, w_in, qk_w, qk_w, tm, tn, functools.partial(_epi_norm_rope, scale=1.0),
                      rope_extras(k_norm), [F32, BF16], "proj_k")
    v32, vb = _matmul(h, w_in, 2 * qk_w, qk_w, tm, tn, _epi_plain, [], [F32, BF16], "proj_v")
    (u,) = _matmul(h, w_in, 3 * qk_w, d, tm, tn, _epi_plain, [], [F32], "proj_u")
    (vs,) = _matmul(h, w_in, 3 * qk_w + d, d, tm, tn, _epi_plain, [], [F32], "proj_vs")
    (gate,) = _matmul(h, w_in, 3 * qk_w + 2 * d, 2 * d, tm, tn, _epi_gate,
                      [(b_gate.reshape(1, 2 * d), (1, tn), lambda n, i, k: (0, n))], [F32], "proj_gate")

    o_attn = attend(q, kb, vb)
    res = _sgu_merge(vs, u, gate, o_attn, sgu_norm, w_s, b_s, seg, want_v)
    merged = res[0]
    res_spec = lambda a: [(a, (tm, tn), lambda n, i, k: (i, n))]
    (x1,) = _matmul(merged, w_o, 0, d, tm, tn, _epi_residual, res_spec(x), [F32], "proj_out")
    hf = _rmsnorm(x1, norm_ffn, min(tm, 256))
    (hid,) = _matmul(hf, w_up, 0, w_up.shape[1], tm, tn, _epi_relu2, [], [BF16], "ffn_up")
    (y,) = _matmul(hid, w_down, 0, d, tm, tn, _epi_residual, res_spec(x1), [F32], "ffn_down")
    return y, k32, v32, (res[1] if want_v else None)


def kernel(x_prompt, x_sample, cache_k_attn, cache_v_attn, norm_mix, w_in, b_gate, q_norm, k_norm, lambda_q1, lambda_k1, lambda_q2, lambda_k2, subln, sgu_norm, w_s, b_s, w_o, norm_ffn, w_up, w_down):
    depth = w_in.shape[0]
    bp, tp, d = x_prompt.shape
    bs, ts, _ = x_sample.shape
    hd2 = 2 * HEAD_DIM
    pos_p = jnp.arange(tp, dtype=jnp.int32)
    pos_s = jnp.tile(PAST_LEN + jnp.arange(ts, dtype=jnp.int32), bs)
    xp = x_prompt.reshape(bp * tp, d)
    xs = x_sample.reshape(bs * ts, d)
    kp_l, vp_l, ks_l, vs_l, sg_l = [], [], [], [], []
    for l in range(depth):
        lambda_init = 0.8 - 0.6 * math.exp(-0.3 * l)
        lam_vecs = [a[l].reshape(1, HEAD_DIM) for a in (lambda_q1, lambda_k1, lambda_q2, lambda_k2)]
        sub = subln[l].reshape(1, hd2)
        wts = (norm_mix[l], w_in[l].astype(BF16), b_gate[l], q_norm[l], k_norm[l], sub, sgu_norm[l],
               w_s[l], b_s[l], w_o[l].astype(BF16), norm_ffn[l], w_up[l].astype(BF16), w_down[l].astype(BF16))
        ck, cv = cache_k_attn[l], cache_v_attn[l]

        attend_p = lambda q, k, v: _attn_prompt(q, k, v, lam_vecs, sub, lambda_init, bp, min(512, tp))
        attend_s = lambda q, k, v: _attn_sample(q, k, v, ck, cv, lam_vecs, sub, lambda_init, ts)
        xp, kp, vp, _ = _layer(xp, pos_p, attend_p, min(tp, SGU_CHUNK), False, 512, lambda_init, wts)
        xs, ksm, vsm, sgv = _layer(xs, pos_s, attend_s, min(ts, SGU_CHUNK), True, bs * ts, lambda_init, wts)
        kp_l.append(kp.reshape(bp, tp, N_HEADS, hd2))
        vp_l.append(vp.reshape(bp, tp, N_HEADS, hd2))
        ks_l.append(ksm.reshape(bs, ts, N_HEADS, hd2))
        vs_l.append(vsm.reshape(bs, ts, N_HEADS, hd2))
        sg_l.append(sgv.reshape(bs, ts, d))
    return (xp.reshape(bp, tp, d), xs.reshape(bs, ts, d), jnp.stack(kp_l), jnp.stack(vp_l),
            jnp.stack(ks_l), jnp.stack(vs_l), jnp.stack(sg_l))
```

```python
import functools
import math

import jax
import jax.numpy as jnp
from jax import lax
from jax.experimental import pallas as pl
from jax.experimental.pallas import tpu as pltpu

D_MODEL = 4096
BATCH = 2
SEQ = 4096
DEC_BATCH = 16
DEC_SEQ = 16
PAST_LEN = 4096
CHUNK = 64
N_HEADS = 16
HEAD_DIM = 128
SGU_CHUNK = 128
SGU_GROUPS = 8
D_FF = 4 * D_MODEL
ROPE_THETA = 10000.0
EPS = 1e-6
NEG_INF = -1e30

LANES = 128
SUBLANES = 8
BF16_ROWS = 16
MXU_WIDTH = 256
EPILOGUE_ROWS = 512
VMEM_LIMIT_BYTES = 56 * 1024 * 1024
MAX_TK = 4096
SAMPLE_CACHE_BLOCK = 512
ATTN_BLOCK = 512
PROMPT_TILES = ((1024, 512), (512, 1024))
DECODE_TN = 1024
LOG2E = math.log2(math.e)
F32 = jnp.float32
BF16 = jnp.bfloat16


def _params(*sem):
    return pltpu.CompilerParams(dimension_semantics=sem, vmem_limit_bytes=VMEM_LIMIT_BYTES)


def _rmsnorm_kernel(x_ref, g_ref, o_ref):
    x = x_ref[...]
    ms = jnp.mean(x * x, axis=-1, keepdims=True)
    o_ref[...] = (x * lax.rsqrt(ms + EPS) * g_ref[...]).astype(o_ref.dtype)


def _rmsnorm(x, g, tm):
    m, d = x.shape
    return pl.pallas_call(
        _rmsnorm_kernel,
        grid=(m // tm,),
        in_specs=[pl.BlockSpec((tm, d), lambda i: (i, 0)), pl.BlockSpec((1, d), lambda i: (0, 0))],
        out_specs=pl.BlockSpec((tm, d), lambda i: (i, 0)),
        out_shape=jax.ShapeDtypeStruct((m, d), BF16),
        compiler_params=_params("parallel"),
        name="rmsnorm",
    )(x, g.reshape(1, d))


def _mm_kernel(x_ref, w_ref, *refs, nk, n_extra, n_out, epilogue):
    extras, outs = refs[:n_extra], refs[n_extra:n_extra + n_out]
    if nk == 1:
        acc = jnp.dot(x_ref[...], w_ref[...], preferred_element_type=F32)
        epilogue(acc, *extras, *outs)
        return
    acc_ref = refs[n_extra + n_out]
    k = pl.program_id(2)

    @pl.when(k == 0)
    def _():
        acc_ref[...] = jnp.zeros_like(acc_ref)

    acc_ref[...] += jnp.dot(x_ref[...], w_ref[...], preferred_element_type=F32)

    @pl.when(k == nk - 1)
    def _():
        epilogue(acc_ref[...], *extras, *outs)


def _matmul(x, w, col0, ncols, tm, tn, epilogue, extras, outs, name):
    m, kdim = x.shape
    tk = min(kdim, MAX_TK)
    nk = kdim // tk
    assert col0 % tn == 0 and ncols % tn == 0 and m % tm == 0 and kdim % tk == 0
    cb = col0 // tn
    in_specs = [pl.BlockSpec((tm, tk), lambda n, i, k: (i, k)),
                pl.BlockSpec((tk, tn), lambda n, i, k: (k, n + cb))]
    in_specs += [pl.BlockSpec(bs, im) for _, bs, im in extras]
    out_specs = [pl.BlockSpec((tm, tn), lambda n, i, k: (i, n)) for _ in outs]
    out_shape = [jax.ShapeDtypeStruct((m, ncols), dt) for dt in outs]
    scratch = [pltpu.VMEM((tm, tn), F32)] if nk > 1 else []
    res = pl.pallas_call(
        functools.partial(_mm_kernel, nk=nk, n_extra=len(extras), n_out=len(outs), epilogue=epilogue),
        grid=(ncols // tn, m // tm, nk),
        in_specs=in_specs,
        out_specs=out_specs,
        out_shape=out_shape,
        scratch_shapes=scratch,
        compiler_params=_params("parallel", "parallel", "arbitrary"),
        name=name,
    )(x, w, *[a for a, _, _ in extras])
    return res


def _mm_cast_kernel(x_ref, w_ref, *refs, n_extra, n_out, epilogue, ride):
    extras = refs[:n_extra]
    pos = n_extra + (1 if ride else 0)
    outs, w16_ref = refs[pos:pos + n_out], refs[pos + n_out]
    if ride:
        refs[pos + n_out + 1][...] = refs[n_extra][...].astype(BF16)

    @pl.when(pl.program_id(1) == 0)
    def _():
        w16_ref[...] = w_ref[...].astype(BF16)

    tm = x_ref.shape[0]
    rows = min(EPILOGUE_ROWS, tm)
    for r0 in range(0, tm, rows):
        part = [r.at[r0:r0 + rows, :] if r.shape[0] == tm else r for r in (*extras, *outs)]
        acc = jnp.dot(x_ref[r0:r0 + rows, :], w16_ref[...], preferred_element_type=F32)
        epilogue(acc, *part)


def _matmul_cast(x, w, col0, ncols, tm, tn, epilogue, extras, outs, name, ride=None):
    m, kdim = x.shape
    assert col0 % tn == 0 and ncols % tn == 0 and m % tm == 0 and kdim <= MAX_TK
    cb = col0 // tn
    nn, nm = ncols // tn, m // tm
    in_specs = [pl.BlockSpec((tm, kdim), lambda n, i, k: (i, 0)),
                pl.BlockSpec((kdim, tn), lambda n, i, k: (0, n + cb))]
    in_specs += [pl.BlockSpec(bs, im) for _, bs, im in extras]
    out_specs = [pl.BlockSpec((tm, tn), lambda n, i, k: (i, n)) for _ in outs]
    out_specs.append(pl.BlockSpec((kdim, tn), lambda n, i, k: (0, n)))
    out_shape = [jax.ShapeDtypeStruct((m, ncols), dt) for dt in outs]
    out_shape.append(jax.ShapeDtypeStruct((kdim, ncols), BF16))
    operands = [x, w] + [a for a, _, _ in extras]
    if ride is not None:
        rrows, rcols = ride.shape
        assert rrows % (nn * nm * BF16_ROWS) == 0
        step_rows = rrows // (nn * nm)
        ride_spec = pl.BlockSpec((step_rows, rcols), lambda n, i, k: (n * nm + i, 0))
        in_specs.append(ride_spec)
        out_specs.append(ride_spec)
        out_shape.append(jax.ShapeDtypeStruct(ride.shape, BF16))
        operands.append(ride)
    return pl.pallas_call(
        functools.partial(_mm_cast_kernel, n_extra=len(extras), n_out=len(outs), epilogue=epilogue,
                          ride=ride is not None),
        grid=(nn, nm, 1),
        in_specs=in_specs,
        out_specs=out_specs,
        out_shape=out_shape,
        compiler_params=_params("parallel", "arbitrary", "arbitrary"),
        name=name,
    )(*operands)


def _epi_plain(acc, *outs):
    for o in outs:
        o[...] = acc.astype(o.dtype)


def _epi_norm_rope(acc, g_ref, cos_ref, sin_ref, *outs, scale):
    g = g_ref[...]
    cos = cos_ref[...]
    sin = sin_ref[...]
    width = min(MXU_WIDTH, acc.shape[1])
    row = lax.broadcasted_iota(jnp.int32, (width, width), 0) // HEAD_DIM
    col = lax.broadcasted_iota(jnp.int32, (width, width), 1) // HEAD_DIM
    same_head = jnp.where(row == col, 1.0, 0.0).astype(BF16)
    for c0 in range(0, acc.shape[1], width):
        z = acc[:, c0:c0 + width]
        ss = jnp.dot((z * z).astype(BF16), same_head, preferred_element_type=F32)
        normed = z * lax.rsqrt(ss * (1.0 / HEAD_DIM) + EPS)
        for j in range(0, width, HEAD_DIM):
            y = normed[:, j:j + HEAD_DIM] * g
            y = y * cos + pltpu.roll(y, HEAD_DIM // 2, axis=1) * sin
            if scale != 1.0:
                y = y * scale
            for o in outs:
                o[:, c0 + j:c0 + j + HEAD_DIM] = y.astype(o.dtype)


def _epi_gate(acc, b_ref, o_ref):
    o_ref[...] = jax.nn.sigmoid(acc + b_ref[...]).astype(o_ref.dtype)


def _epi_residual(acc, r_ref, o_ref):
    o_ref[...] = (r_ref[...] + acc).astype(o_ref.dtype)


def _epi_relu2(acc, o_ref):
    o_ref[...] = jnp.square(jnp.maximum(acc, 0.0)).astype(o_ref.dtype)


def _rope_tables(pos):
    half = HEAD_DIM // 2
    inv = ROPE_THETA ** (-jnp.arange(half, dtype=F32) / half)
    ang = pos.astype(F32)[:, None] * inv[None, :]
    cos, sin = jnp.cos(ang), jnp.sin(ang)
    return jnp.concatenate([cos, cos], axis=-1), jnp.concatenate([-sin, sin], axis=-1)


def _lambda(lq1, lk1, lq2, lk2, lambda_init):
    s1 = jnp.sum(lq1[...] * lk1[...], axis=-1, keepdims=True)
    s2 = jnp.sum(lq2[...] * lk2[...], axis=-1, keepdims=True)
    return jnp.exp(s1) - jnp.exp(s2) + lambda_init


def _subln(o, subln_ref, lambda_init):
    ms = jnp.mean(o * o, axis=-1, keepdims=True)
    return o * lax.rsqrt(ms + EPS) * subln_ref[...] * (1.0 - lambda_init)


def _attn_prompt_kernel(lq1, lk1, lq2, lk2, subln_ref, bias_ref, q_ref, k_ref, v_ref, o_ref,
                        s_ref, acc_ref, mp_ref, mb_ref, lp_ref, *, tq, lambda_init):
    qi = pl.program_id(2)
    d = HEAD_DIM
    nl = tq // LANES
    nt = (((1,), (1,)), ((), ()))
    mp_ref[...] = jnp.full_like(mp_ref, NEG_INF)

    def scores(j, masked):
        off = pl.multiple_of(j * tq, tq)
        for c in range(2):
            s = lax.dot_general(q_ref[:, c * d:(c + 1) * d], k_ref[pl.ds(off, tq), c * d:(c + 1) * d], nt,
                                preferred_element_type=F32)
            if masked:
                s = s + bias_ref[...]
            s_ref[c, j] = s
            m = mp_ref[c]
            for t in range(nl):
                m = jnp.maximum(m, s[:, t * LANES:(t + 1) * LANES])
            mp_ref[c] = m

    def scores_body(j, carry):
        scores(j, False)
        return carry

    lax.fori_loop(0, qi, scores_body, 0)
    scores(qi, True)

    for c in range(2):
        mb_ref[c] = jnp.broadcast_to(jnp.max(mp_ref[c], axis=-1, keepdims=True), (tq, LANES))
    lp_ref[...] = jnp.zeros_like(lp_ref)
    acc_ref[...] = jnp.zeros_like(acc_ref)

    def accumulate(j, carry):
        off = pl.multiple_of(j * tq, tq)
        vb = v_ref[pl.ds(off, tq), :]
        for c in range(2):
            s = s_ref[c, j]
            mb = mb_ref[c]
            ps = [jnp.exp2(s[:, t * LANES:(t + 1) * LANES] - mb) for t in range(nl)]
            lp = lp_ref[c]
            for p in ps:
                lp = lp + p
            lp_ref[c] = lp
            pb = jnp.concatenate(ps, axis=1).astype(BF16)
            acc_ref[c] += jnp.dot(pb, vb, preferred_element_type=F32)
        return carry

    lax.fori_loop(0, qi + 1, accumulate, 0)

    lam = _lambda(lq1, lk1, lq2, lk2, lambda_init)
    l0 = jnp.sum(lp_ref[0], axis=-1, keepdims=True)
    l1 = jnp.sum(lp_ref[1], axis=-1, keepdims=True)
    o = acc_ref[0] / l0 - lam * (acc_ref[1] / l1)
    o_ref[...] = _subln(o, subln_ref, lambda_init).astype(o_ref.dtype)


def _attn_prompt(q, k, v, lam_vecs, subln, lambda_init, batch, tq):
    mt, width = q.shape
    t = mt // batch
    hd2 = 2 * HEAD_DIM
    nh = width // hd2
    nq = t // tq
    chunk_of = jnp.arange(tq, dtype=jnp.int32) // CHUNK
    bias = jnp.where(chunk_of[None, :] <= chunk_of[:, None], 0.0, NEG_INF).astype(F32)
    vec = pl.BlockSpec((1, HEAD_DIM), lambda b, h, i: (0, 0))
    return pl.pallas_call(
        functools.partial(_attn_prompt_kernel, tq=tq, lambda_init=lambda_init),
        grid=(batch, nh, nq),
        in_specs=[vec, vec, vec, vec,
                  pl.BlockSpec((1, hd2), lambda b, h, i: (0, 0)),
                  pl.BlockSpec((tq, tq), lambda b, h, i: (0, 0)),
                  pl.BlockSpec((tq, hd2), lambda b, h, i: (b * nq + i, h)),
                  pl.BlockSpec((t, hd2), lambda b, h, i: (b, h)),
                  pl.BlockSpec((t, hd2), lambda b, h, i: (b, h))],
        out_specs=pl.BlockSpec((tq, hd2), lambda b, h, i: (b * nq + i, h)),
        out_shape=jax.ShapeDtypeStruct((mt, width), F32),
        scratch_shapes=[pltpu.VMEM((2, nq, tq, tq), F32), pltpu.VMEM((2, tq, hd2), F32),
                        pltpu.VMEM((2, tq, LANES), F32), pltpu.VMEM((2, tq, LANES), F32),
                        pltpu.VMEM((2, tq, LANES), F32)],
        compiler_params=_params("parallel", "parallel", "arbitrary"),
        name="attn_prompt",
    )(*lam_vecs, subln, bias, q, k, v)


def _transpose_sublanes(x):
    v = list(x)
    sub = lax.broadcasted_iota(jnp.int32, (SUBLANES, LANES), 0)
    s = SUBLANES // 2
    while s:
        low = (sub & s) == 0
        nxt = list(v)
        for i in range(SUBLANES):
            if not i & s:
                a, b = v[i], v[i + s]
                nxt[i] = jnp.where(low, a, pltpu.roll(b, s, axis=0))
                nxt[i + s] = jnp.where(low, pltpu.roll(a, SUBLANES - s, axis=0), b)
        v = nxt
        s //= 2
    return v


def _heads_to_rows(src_refs, dst_ref, rows):
    tp, ngroups = src_refs[0].shape[:2]
    assert rows % SUBLANES == 0

    def body(t, carry):
        p0 = pl.multiple_of(t * rows, rows)
        for c, ref in enumerate(src_refs):
            for hh in range(ngroups):
                x = ref[pl.ds(p0, rows), hh]
                parts = [_transpose_sublanes([x[g + i] for i in range(SUBLANES)])
                         for g in range(0, rows, SUBLANES)]
                for j in range(SUBLANES):
                    tile = jnp.concatenate([part[j] for part in parts], axis=0).astype(dst_ref.dtype)
                    dst_ref[hh * SUBLANES + j, pl.ds(p0, rows), c * LANES:(c + 1) * LANES] = tile
        return carry

    lax.fori_loop(0, tp // rows, body, 0)


def _attn_sample_kernel(lq1, lk1, lq2, lk2, subln_ref, q_ref, kn_ref, vn_ref, k0_ref, k1_ref, v0_ref, v1_ref,
                        o_ref, m_ref, l_ref, acc_ref, kt_ref, vt_ref, s_ref, p_ref,
                        *, past_len, nheads, lambda_init):
    d = HEAD_DIM
    tq = q_ref.shape[0]
    tp = k0_ref.shape[0]
    pc = pl.program_id(1)
    nt = (((1,), (1,)), ((), ()))

    @pl.when(pc == 0)
    def _():
        m_ref[...] = jnp.full_like(m_ref, NEG_INF)
        l_ref[...] = jnp.zeros_like(l_ref)
        acc_ref[...] = jnp.zeros_like(acc_ref)

    def stacked_q(h):
        q = q_ref[:, h * 2 * d:(h + 1) * 2 * d]
        z = jnp.zeros((tq, d), q.dtype)
        return jnp.concatenate([jnp.concatenate([q[:, :d], z], axis=1),
                                jnp.concatenate([z, q[:, d:]], axis=1)], axis=0)

    def qchunk(shape):
        return (past_len + lax.broadcasted_iota(jnp.int32, shape, 0) % tq) // CHUNK

    def softmax_step(s, valid):
        m_prev = m_ref[...]
        m_new = jnp.maximum(m_prev, jnp.max(s, axis=-1, keepdims=True))
        alpha = jnp.exp2(m_prev - m_new)
        p = jnp.exp2(s - m_new)
        if valid is not None:
            p = jnp.where(valid, p, 0.0)
        l_ref[...] = alpha * l_ref[...] + jnp.sum(p, axis=-1, keepdims=True)
        m_ref[...] = m_new
        acc_ref[...] = alpha * acc_ref[...]
        return p.astype(BF16)

    _heads_to_rows((k0_ref, k1_ref), kt_ref, BF16_ROWS)
    _heads_to_rows((v0_ref, v1_ref), vt_ref, BF16_ROWS)

    rows = 2 * tq
    for h in range(nheads):
        s_ref[h * rows:(h + 1) * rows, :] = lax.dot_general(stacked_q(h), kt_ref[h], nt,
                                                            preferred_element_type=F32)
    shape = (nheads * rows, tp)
    vis = (pc * tp + lax.broadcasted_iota(jnp.int32, shape, 1)) // CHUNK <= qchunk(shape)
    p_ref[...] = softmax_step(jnp.where(vis, s_ref[...], NEG_INF), None)
    for h in range(nheads):
        sl = slice(h * rows, (h + 1) * rows)
        acc_ref[sl, :] += jnp.dot(p_ref[sl, :], vt_ref[h], preferred_element_type=F32)

    @pl.when(pc == pl.num_programs(1) - 1)
    def _():
        lam = _lambda(lq1, lk1, lq2, lk2, lambda_init)
        pad = jnp.zeros((LANES - tq, 2 * d), kn_ref.dtype)
        vn = []
        for h in range(nheads):
            hs = slice(h * 2 * d, (h + 1) * 2 * d)
            kn = jnp.concatenate([kn_ref[:, hs], pad], axis=0)
            vn.append(jnp.concatenate([vn_ref[:, hs], pad], axis=0))
            s_ref[h * rows:(h + 1) * rows, :LANES] = lax.dot_general(stacked_q(h), kn, nt,
                                                                     preferred_element_type=F32)
        shape_n = (nheads * rows, LANES)
        col = lax.broadcasted_iota(jnp.int32, shape_n, 1)
        valid = col < tq
        vis_n = valid & ((past_len + col) // CHUNK <= qchunk(shape_n))
        p_n = softmax_step(jnp.where(vis_n, s_ref[:, :LANES], NEG_INF), valid)
        for h in range(nheads):
            sl = slice(h * rows, (h + 1) * rows)
            o2 = (acc_ref[sl, :] + jnp.dot(p_n[sl, :], vn[h], preferred_element_type=F32)) / l_ref[sl, :]
            o = o2[:tq] - lam * o2[tq:]
            o_ref[:, h * 2 * d:(h + 1) * 2 * d] = _subln(o, subln_ref, lambda_init).astype(o_ref.dtype)


def _attn_sample(q, kn, vn, cache_k, cache_v, lam_vecs, subln, lambda_init, tq):
    mt, width = q.shape
    nb, past_len, nh, hd2 = cache_k.shape
    tp = min(SAMPLE_CACHE_BLOCK, past_len)
    assert nh % SUBLANES == 0 and tp % SUBLANES == 0 and past_len % tp == 0
    cache_k = cache_k.reshape(nb, past_len, nh // SUBLANES, SUBLANES, hd2)
    cache_v = cache_v.reshape(nb, past_len, nh // SUBLANES, SUBLANES, hd2)
    vec = pl.BlockSpec((1, HEAD_DIM), lambda b, p: (0, 0))
    row = pl.BlockSpec((tq, width), lambda b, p: (b, 0))
    half = [pl.BlockSpec((None, tp, nh // SUBLANES, SUBLANES, HEAD_DIM),
                         functools.partial(lambda b, p, c: (b, p, 0, 0, c), c=c)) for c in range(2)]
    return pl.pallas_call(
        functools.partial(_attn_sample_kernel, past_len=past_len, nheads=nh, lambda_init=lambda_init),
        grid=(nb, past_len // tp),
        in_specs=[vec, vec, vec, vec, pl.BlockSpec((1, hd2), lambda b, p: (0, 0)),
                  row, row, row, half[0], half[1], half[0], half[1]],
        out_specs=row,
        out_shape=jax.ShapeDtypeStruct((mt, width), F32),
        scratch_shapes=[pltpu.VMEM((nh * 2 * tq, 1), F32), pltpu.VMEM((nh * 2 * tq, 1), F32),
                        pltpu.VMEM((nh * 2 * tq, hd2), F32),
                        pltpu.VMEM((nh, tp, hd2), BF16), pltpu.VMEM((nh, tp, hd2), BF16),
                        pltpu.VMEM((nh * 2 * tq, tp), F32), pltpu.VMEM((nh * 2 * tq, tp), BF16)],
        compiler_params=_params("parallel", "arbitrary"),
        name="attn_sample",
    )(*lam_vecs, subln, q, kn, vn, cache_k, cache_k, cache_v, cache_v)


def _sgu_merge_kernel(vs_ref, u_ref, ga_ref, gb_ref, oa_ref, gn_ref, w_ref, bt_ref, *outs,
                      seg, groups, want_v):
    vs = vs_ref[...]
    ms = jnp.mean(vs * vs, axis=-1, keepdims=True)
    vn = vs * lax.rsqrt(ms + EPS) * gn_ref[...]
    if want_v:
        outs[1][...] = vn
    rows = vs.shape[0]
    gd = vs.shape[1] // groups
    r = lax.broadcasted_iota(jnp.int32, (rows, rows), 0)
    c = lax.broadcasted_iota(jnp.int32, (rows, rows), 1)
    keep = (r // seg == c // seg) & (c <= r)
    vb = vn.astype(BF16)
    for g in range(groups):
        sl = slice(g * gd, (g + 1) * gd)
        w = jnp.where(keep, w_ref[g], 0.0).astype(BF16)
        s = jnp.dot(w, vb[:, sl], preferred_element_type=F32) + bt_ref[:, g:g + 1]
        o_sgu = u_ref[:, sl] * s
        outs[0][:, sl] = (ga_ref[:, sl] * oa_ref[:, sl] + gb_ref[:, sl] * o_sgu).astype(outs[0].dtype)


def _sgu_merge(vs, u, gate, o_attn, sgu_norm, w_s, b_s, seg, want_v):
    m, d = vs.shape
    rows = SGU_CHUNK
    groups = w_s.shape[0]
    rep = rows // seg
    w_t = jnp.tile(w_s[:, :seg, :seg], (1, rep, rep))
    b_t = jnp.tile(jnp.swapaxes(b_s[:, :seg], 0, 1), (rep, 1))
    blk = pl.BlockSpec((rows, d), lambda i: (i, 0))
    out_shape = [jax.ShapeDtypeStruct((m, d), BF16)]
    if want_v:
        out_shape.append(jax.ShapeDtypeStruct((m, d), F32))
    return pl.pallas_call(
        functools.partial(_sgu_merge_kernel, seg=seg, groups=groups, want_v=want_v),
        grid=(m // rows,),
        in_specs=[blk, blk, blk, pl.BlockSpec((rows, d), lambda i: (i, 1)), blk,
                  pl.BlockSpec((1, d), lambda i: (0, 0)),
                  pl.BlockSpec((groups, rows, rows), lambda i: (0, 0, 0)),
                  pl.BlockSpec((rows, groups), lambda i: (0, 0))],
        out_specs=[blk] * len(out_shape),
        out_shape=out_shape,
        compiler_params=_params("parallel"),
        name="sgu_merge",
    )(vs, u, gate, gate, o_attn, sgu_norm.reshape(1, d), w_t, b_t)


def _layer(x, pos, attend, seg, want_v, tiles, lambda_init, wts, w16):
    (norm_mix, w_in, b_gate, q_norm, k_norm, subln, sgu_norm, w_s, b_s, w_o, norm_ffn, w_up, w_down) = wts
    m, d = x.shape
    qk_w = N_HEADS * 2 * HEAD_DIM
    (tm, tn), (tm_down, tn_down) = tiles
    tn, tn_down = min(tn, d), min(tn_down, d)
    rounding = not w16
    cos, sin = _rope_tables(pos)

    def mm(name, a, w, col0, ncols, epilogue, extras, outs, ride=None):
        if not rounding:
            return _matmul(a, w16[name], 0, ncols, tm, tn, epilogue, extras, outs, name)
        res = _matmul_cast(a, w, col0, ncols, tm, tn, epilogue, extras, outs, name,
                           None if ride is None else ride[1])
        w16[name] = res[len(outs)]
        if ride is not None:
            w16[ride[0]] = res[len(outs) + 1]
        return res[:len(outs)]

    def rope_extras(g, rows):
        npos = pos.shape[0] // rows
        return [(g.reshape(1, HEAD_DIM), (1, HEAD_DIM), lambda n, i, k: (0, 0)),
                (cos, (rows, HEAD_DIM), lambda n, i, k: (i % npos, 0)),
                (sin, (rows, HEAD_DIM), lambda n, i, k: (i % npos, 0))]

    def res_extras(a, rows, cols):
        return [(a, (rows, cols), lambda n, i, k: (i, n))]

    h = _rmsnorm(x, norm_mix, min(tm, 256))
    (q,) = mm("proj_q", h, w_in, 0, qk_w, functools.partial(_epi_norm_rope, scale=HEAD_DIM ** -0.5 * LOG2E),
              rope_extras(q_norm, tm), [BF16])
    k32, kb = mm("proj_k", h, w_in, qk_w, qk_w, functools.partial(_epi_norm_rope, scale=1.0),
                 rope_extras(k_norm, tm), [F32, BF16])
    v32, vb = mm("proj_v", h, w_in, 2 * qk_w, qk_w, _epi_plain, [], [F32, BF16])
    (u,) = mm("proj_u", h, w_in, 3 * qk_w, d, _epi_plain, [], [F32])
    (vs,) = mm("proj_vs", h, w_in, 3 * qk_w + d, d, _epi_plain, [], [F32])
    (gate,) = mm("proj_gate", h, w_in, 3 * qk_w + 2 * d, 2 * d, _epi_gate,
                 [(b_gate.reshape(1, 2 * d), (1, tn), lambda n, i, k: (0, n))], [F32])

    o_attn = attend(q, kb, vb)
    res = _sgu_merge(vs, u, gate, o_attn, sgu_norm, w_s, b_s, seg, want_v)
    merged = res[0]
    (x1,) = mm("proj_out", merged, w_o, 0, d, _epi_residual, res_extras(x, tm, tn), [F32])
    hf = _rmsnorm(x1, norm_ffn, min(tm, 256))
    (hid,) = mm("ffn_up", hf, w_up, 0, w_up.shape[1], _epi_relu2, [], [BF16], ride=("ffn_down", w_down))
    (y,) = _matmul(hid, w16["ffn_down"], 0, d, tm_down, tn_down, _epi_residual,
                   res_extras(x1, tm_down, tn_down), [F32], "ffn_down")
    return y, k32, v32, (res[1] if want_v else None)


def kernel(x_prompt, x_sample, cache_k_attn, cache_v_attn, norm_mix, w_in, b_gate, q_norm, k_norm, lambda_q1, lambda_k1, lambda_q2, lambda_k2, subln, sgu_norm, w_s, b_s, w_o, norm_ffn, w_up, w_down):
    depth = w_in.shape[0]
    bp, tp, d = x_prompt.shape
    bs, ts, _ = x_sample.shape
    hd2 = 2 * HEAD_DIM
    pos_p = jnp.arange(tp, dtype=jnp.int32)
    pos_s = jnp.tile(PAST_LEN + jnp.arange(ts, dtype=jnp.int32), bs)
    xp = x_prompt.reshape(bp * tp, d)
    xs = x_sample.reshape(bs * ts, d)
    kp_l, vp_l, ks_l, vs_l, sg_l = [], [], [], [], []
    for l in range(depth):
        lambda_init = 0.8 - 0.6 * math.exp(-0.3 * l)
        lam_vecs = [a[l].reshape(1, HEAD_DIM) for a in (lambda_q1, lambda_k1, lambda_q2, lambda_k2)]
        sub = subln[l].reshape(1, hd2)
        wts = (norm_mix[l], w_in[l], b_gate[l], q_norm[l], k_norm[l], sub, sgu_norm[l],
               w_s[l], b_s[l], w_o[l], norm_ffn[l], w_up[l], w_down[l])
        ck, cv = cache_k_attn[l], cache_v_attn[l]

        attend_p = lambda q, k, v: _attn_prompt(q, k, v, lam_vecs, sub, lambda_init, bp, min(ATTN_BLOCK, tp))
        attend_s = lambda q, k, v: _attn_sample(q, k, v, ck, cv, lam_vecs, sub, lambda_init, ts)
        w16 = {}
        xp, kp, vp, _ = _layer(xp, pos_p, attend_p, min(tp, SGU_CHUNK), False, PROMPT_TILES, lambda_init, wts, w16)
        decode_tiles = ((bs * ts, DECODE_TN), (bs * ts, DECODE_TN))
        xs, ksm, vsm, sgv = _layer(xs, pos_s, attend_s, min(ts, SGU_CHUNK), True, decode_tiles, lambda_init, wts, w16)
        kp_l.append(kp.reshape(bp, tp, N_HEADS, hd2))
        vp_l.append(vp.reshape(bp, tp, N_HEADS, hd2))
        ks_l.append(ksm.reshape(bs, ts, N_HEADS, hd2))
        vs_l.append(vsm.reshape(bs, ts, N_HEADS, hd2))
        sg_l.append(sgv.reshape(bs, ts, d))
    return (xp.reshape(bp, tp, d), xs.reshape(bs, ts, d), jnp.stack(kp_l), jnp.stack(vp_l),
            jnp.stack(ks_l), jnp.stack(vs_l), jnp.stack(sg_l))
```

```python
import functools
import math

import jax
import jax.numpy as jnp
from jax import lax
from jax.experimental import pallas as pl
from jax.experimental.pallas import tpu as pltpu

D_MODEL = 4096
BATCH = 2
SEQ = 4096
DEC_BATCH = 16
DEC_SEQ = 16
PAST_LEN = 4096
CHUNK = 64
N_HEADS = 16
HEAD_DIM = 128
SGU_CHUNK = 128
SGU_GROUPS = 8
D_FF = 4 * D_MODEL
ROPE_THETA = 10000.0
EPS = 1e-6
NEG_INF = -1e30

LANES = 128
SUBLANES = 8
BF16_ROWS = 16
MXU_WIDTH = 256
EPILOGUE_ROWS = 256
VMEM_LIMIT_BYTES = 56 * 1024 * 1024
MAX_TK = 4096
SAMPLE_CACHE_BLOCK = 512
ATTN_BLOCK = 512
PROMPT_TILES = ((512, 1024), (512, 1024))
DECODE_TN = 1024
LOG2E = math.log2(math.e)
F32 = jnp.float32
BF16 = jnp.bfloat16


def _params(*sem):
    return pltpu.CompilerParams(dimension_semantics=sem, vmem_limit_bytes=VMEM_LIMIT_BYTES)


def _rmsnorm_kernel(x_ref, g_ref, o_ref):
    x = x_ref[...]
    ms = jnp.mean(x * x, axis=-1, keepdims=True)
    o_ref[...] = (x * lax.rsqrt(ms + EPS) * g_ref[...]).astype(o_ref.dtype)


def _rmsnorm(x, g, tm):
    m, d = x.shape
    return pl.pallas_call(
        _rmsnorm_kernel,
        grid=(m // tm,),
        in_specs=[pl.BlockSpec((tm, d), lambda i: (i, 0)), pl.BlockSpec((1, d), lambda i: (0, 0))],
        out_specs=pl.BlockSpec((tm, d), lambda i: (i, 0)),
        out_shape=jax.ShapeDtypeStruct((m, d), BF16),
        compiler_params=_params("parallel"),
        name="rmsnorm",
    )(x, g.reshape(1, d))


def _mm_kernel(x_ref, w_ref, *refs, nk, n_extra, n_out, epilogue):
    extras, outs = refs[:n_extra], refs[n_extra:n_extra + n_out]
    if nk == 1:
        acc = jnp.dot(x_ref[...], w_ref[...], preferred_element_type=F32)
        epilogue(acc, *extras, *outs)
        return
    acc_ref = refs[n_extra + n_out]
    k = pl.program_id(2)

    @pl.when(k == 0)
    def _():
        acc_ref[...] = jnp.zeros_like(acc_ref)

    acc_ref[...] += jnp.dot(x_ref[...], w_ref[...], preferred_element_type=F32)

    @pl.when(k == nk - 1)
    def _():
        epilogue(acc_ref[...], *extras, *outs)


def _matmul(x, w, col0, ncols, tm, tn, epilogue, extras, outs, name):
    m, kdim = x.shape
    tk = min(kdim, MAX_TK)
    nk = kdim // tk
    assert col0 % tn == 0 and ncols % tn == 0 and m % tm == 0 and kdim % tk == 0
    cb = col0 // tn
    in_specs = [pl.BlockSpec((tm, tk), lambda n, i, k: (i, k)),
                pl.BlockSpec((tk, tn), lambda n, i, k: (k, n + cb))]
    in_specs += [pl.BlockSpec(bs, im) for _, bs, im in extras]
    out_specs = [pl.BlockSpec((tm, tn), lambda n, i, k: (i, n)) for _ in outs]
    out_shape = [jax.ShapeDtypeStruct((m, ncols), dt) for dt in outs]
    scratch = [pltpu.VMEM((tm, tn), F32)] if nk > 1 else []
    res = pl.pallas_call(
        functools.partial(_mm_kernel, nk=nk, n_extra=len(extras), n_out=len(outs), epilogue=epilogue),
        grid=(ncols // tn, m // tm, nk),
        in_specs=in_specs,
        out_specs=out_specs,
        out_shape=out_shape,
        scratch_shapes=scratch,
        compiler_params=_params("parallel", "parallel", "arbitrary"),
        name=name,
    )(x, w, *[a for a, _, _ in extras])
    return res


def _mm_cast_kernel(x_ref, w_ref, *refs, nn, n_extra, n_out, epilogue, ride):
    extras = refs[:n_extra]
    pos = n_extra + (1 if ride else 0)
    outs, w16_ref = refs[pos:pos + n_out], refs[pos + n_out]
    wb_ref = refs[-1]
    n, i = pl.program_id(0), pl.program_id(1)
    chunk_rows = w_ref.shape[0]

    @pl.when(n < nn)
    def _():
        chunk = w_ref[...].astype(BF16)
        wb_ref[n % 2, pl.ds(pl.multiple_of(i * chunk_rows, chunk_rows), chunk_rows), :] = chunk
        w16_ref[...] = chunk

    @pl.when(n > 0)
    def _():
        if ride:
            refs[pos + n_out + 1][...] = refs[n_extra][...].astype(BF16)
        tm = x_ref.shape[0]
        rows = min(EPILOGUE_ROWS, tm)
        w = wb_ref[(n - 1) % 2]
        for r0 in range(0, tm, rows):
            part = [r.at[r0:r0 + rows, :] if r.shape[0] == tm else r for r in (*extras, *outs)]
            acc = jnp.dot(x_ref[r0:r0 + rows, :], w, preferred_element_type=F32)
            epilogue(acc, *part)


def _matmul_cast(x, w, col0, ncols, tm, tn, epilogue, extras, outs, name, ride=None):
    m, kdim = x.shape
    nn, nm = ncols // tn, m // tm
    assert col0 % tn == 0 and ncols % tn == 0 and m % tm == 0 and kdim <= MAX_TK
    assert kdim % (nm * BF16_ROWS) == 0
    cb = col0 // tn
    chunk = kdim // nm

    def row(n, i):
        return jnp.where(n == 0, 0, i)

    def col(n):
        return jnp.maximum(n - 1, 0)

    def loading(n, i):
        return jnp.where(n == nn, nm - 1, i), jnp.minimum(n, nn - 1)

    in_specs = [pl.BlockSpec((tm, kdim), lambda n, i, k: (row(n, i), 0)),
                pl.BlockSpec((chunk, tn), lambda n, i, k: (loading(n, i)[0], loading(n, i)[1] + cb))]
    in_specs += [pl.BlockSpec(bs, functools.partial(lambda n, i, k, im: im(col(n), row(n, i), k), im=im))
                 for _, bs, im in extras]
    out_specs = [pl.BlockSpec((tm, tn), lambda n, i, k: (row(n, i), col(n))) for _ in outs]
    out_specs.append(pl.BlockSpec((chunk, tn), lambda n, i, k: loading(n, i)))
    out_shape = [jax.ShapeDtypeStruct((m, ncols), dt) for dt in outs]
    out_shape.append(jax.ShapeDtypeStruct((kdim, ncols), BF16))
    operands = [x, w] + [a for a, _, _ in extras]
    if ride is not None:
        rrows, rcols = ride.shape
        assert rrows % (nn * nm * BF16_ROWS) == 0
        step_rows = rrows // (nn * nm)
        ride_spec = pl.BlockSpec((step_rows, rcols), lambda n, i, k: (col(n) * nm + row(n, i), 0))
        in_specs.append(ride_spec)
        out_specs.append(ride_spec)
        out_shape.append(jax.ShapeDtypeStruct(ride.shape, BF16))
        operands.append(ride)
    return pl.pallas_call(
        functools.partial(_mm_cast_kernel, nn=nn, n_extra=len(extras), n_out=len(outs), epilogue=epilogue,
                          ride=ride is not None),
        grid=(nn + 1, nm, 1),
        in_specs=in_specs,
        out_specs=out_specs,
        out_shape=out_shape,
        scratch_shapes=[pltpu.VMEM((2, kdim, tn), BF16)],
        compiler_params=_params("arbitrary", "arbitrary", "arbitrary"),
        name=name,
    )(*operands)


def _epi_plain(acc, *outs):
    for o in outs:
        o[...] = acc.astype(o.dtype)


def _epi_norm_rope(acc, g_ref, cos_ref, sin_ref, *outs, scale):
    g = g_ref[...]
    cos = cos_ref[...]
    sin = sin_ref[...]
    width = min(MXU_WIDTH, acc.shape[1])
    row = lax.broadcasted_iota(jnp.int32, (width, width), 0) // HEAD_DIM
    col = lax.broadcasted_iota(jnp.int32, (width, width), 1) // HEAD_DIM
    same_head = jnp.where(row == col, 1.0, 0.0).astype(BF16)
    for c0 in range(0, acc.shape[1], width):
        z = acc[:, c0:c0 + width]
        ss = jnp.dot((z * z).astype(BF16), same_head, preferred_element_type=F32)
        normed = z * lax.rsqrt(ss * (1.0 / HEAD_DIM) + EPS)
        for j in range(0, width, HEAD_DIM):
            y = normed[:, j:j + HEAD_DIM] * g
            y = y * cos + pltpu.roll(y, HEAD_DIM // 2, axis=1) * sin
            if scale != 1.0:
                y = y * scale
            for o in outs:
                o[:, c0 + j:c0 + j + HEAD_DIM] = y.astype(o.dtype)


def _epi_gate(acc, b_ref, o_ref):
    o_ref[...] = jax.nn.sigmoid(acc + b_ref[...]).astype(o_ref.dtype)


def _epi_residual(acc, r_ref, o_ref):
    o_ref[...] = (r_ref[...] + acc).astype(o_ref.dtype)


def _epi_relu2(acc, o_ref):
    o_ref[...] = jnp.square(jnp.maximum(acc, 0.0)).astype(o_ref.dtype)


def _rope_tables(pos):
    half = HEAD_DIM // 2
    inv = ROPE_THETA ** (-jnp.arange(half, dtype=F32) / half)
    ang = pos.astype(F32)[:, None] * inv[None, :]
    cos, sin = jnp.cos(ang), jnp.sin(ang)
    return jnp.concatenate([cos, cos], axis=-1), jnp.concatenate([-sin, sin], axis=-1)


def _lambda(lq1, lk1, lq2, lk2, lambda_init):
    s1 = jnp.sum(lq1[...] * lk1[...], axis=-1, keepdims=True)
    s2 = jnp.sum(lq2[...] * lk2[...], axis=-1, keepdims=True)
    return jnp.exp(s1) - jnp.exp(s2) + lambda_init


def _subln(o, subln_ref, lambda_init):
    ms = jnp.mean(o * o, axis=-1, keepdims=True)
    return o * lax.rsqrt(ms + EPS) * subln_ref[...] * (1.0 - lambda_init)


def _for_each_block(count, fn):
    def pair(t, carry):
        fn(2 * t, 2 * t + 1)
        return carry

    lax.fori_loop(0, count // 2, pair, 0)

    @pl.when(count % 2 == 1)
    def _():
        fn(count - 1)


def _attn_prompt_kernel(lq1, lk1, lq2, lk2, subln_ref, bias_ref, q_ref, k_ref, v_ref, o_ref,
                        s_ref, acc_ref, mp_ref, mb_ref, lp_ref, *, tq, lambda_init):
    qi = pl.program_id(2)
    d = HEAD_DIM
    nl = tq // LANES
    nt = (((1,), (1,)), ((), ()))
    mp_ref[...] = jnp.full_like(mp_ref, NEG_INF)

    def scores(blocks, masked):
        for c in range(2):
            m = mp_ref[c]
            for j in blocks:
                off = pl.multiple_of(j * tq, tq)
                s = lax.dot_general(q_ref[:, c * d:(c + 1) * d], k_ref[pl.ds(off, tq), c * d:(c + 1) * d], nt,
                                    preferred_element_type=F32)
                if masked:
                    s = s + bias_ref[...]
                s_ref[c, j] = s
                for t in range(nl):
                    m = jnp.maximum(m, s[:, t * LANES:(t + 1) * LANES])
            mp_ref[c] = m

    _for_each_block(qi, lambda *blocks: scores(blocks, False))
    scores((qi,), True)

    for c in range(2):
        mb_ref[c] = jnp.broadcast_to(jnp.max(mp_ref[c], axis=-1, keepdims=True), (tq, LANES))
    lp_ref[...] = jnp.zeros_like(lp_ref)
    acc_ref[...] = jnp.zeros_like(acc_ref)

    def accumulate(*blocks):
        for c in range(2):
            mb = mb_ref[c]
            lp = lp_ref[c]
            for j in blocks:
                s = s_ref[c, j]
                ps = [jnp.exp2(s[:, t * LANES:(t + 1) * LANES] - mb) for t in range(nl)]
                for p in ps:
                    lp = lp + p
                pb = jnp.concatenate(ps, axis=1).astype(BF16)
                acc_ref[c] += jnp.dot(pb, v_ref[pl.ds(pl.multiple_of(j * tq, tq), tq), :],
                                      preferred_element_type=F32)
            lp_ref[c] = lp

    _for_each_block(qi + 1, accumulate)

    lam = _lambda(lq1, lk1, lq2, lk2, lambda_init)
    l0 = jnp.sum(lp_ref[0], axis=-1, keepdims=True)
    l1 = jnp.sum(lp_ref[1], axis=-1, keepdims=True)
    o = acc_ref[0] / l0 - lam * (acc_ref[1] / l1)
    o_ref[...] = _subln(o, subln_ref, lambda_init).astype(o_ref.dtype)


def _attn_prompt(q, k, v, lam_vecs, subln, lambda_init, batch, tq):
    mt, width = q.shape
    t = mt // batch
    hd2 = 2 * HEAD_DIM
    nh = width // hd2
    nq = t // tq
    chunk_of = jnp.arange(tq, dtype=jnp.int32) // CHUNK
    bias = jnp.where(chunk_of[None, :] <= chunk_of[:, None], 0.0, NEG_INF).astype(F32)
    vec = pl.BlockSpec((1, HEAD_DIM), lambda b, h, i: (0, 0))
    return pl.pallas_call(
        functools.partial(_attn_prompt_kernel, tq=tq, lambda_init=lambda_init),
        grid=(batch, nh, nq),
        in_specs=[vec, vec, vec, vec,
                  pl.BlockSpec((1, hd2), lambda b, h, i: (0, 0)),
                  pl.BlockSpec((tq, tq), lambda b, h, i: (0, 0)),
                  pl.BlockSpec((tq, hd2), lambda b, h, i: (b * nq + i, h)),
                  pl.BlockSpec((t, hd2), lambda b, h, i: (b, h)),
                  pl.BlockSpec((t, hd2), lambda b, h, i: (b, h))],
        out_specs=pl.BlockSpec((tq, hd2), lambda b, h, i: (b * nq + i, h)),
        out_shape=jax.ShapeDtypeStruct((mt, width), F32),
        scratch_shapes=[pltpu.VMEM((2, nq, tq, tq), F32), pltpu.VMEM((2, tq, hd2), F32),
                        pltpu.VMEM((2, tq, LANES), F32), pltpu.VMEM((2, tq, LANES), F32),
                        pltpu.VMEM((2, tq, LANES), F32)],
        compiler_params=_params("parallel", "parallel", "arbitrary"),
        name="attn_prompt",
    )(*lam_vecs, subln, bias, q, k, v)


def _transpose_sublanes(x):
    v = list(x)
    sub = lax.broadcasted_iota(jnp.int32, (SUBLANES, LANES), 0)
    s = SUBLANES // 2
    while s:
        low = (sub & s) == 0
        nxt = list(v)
        for i in range(SUBLANES):
            if not i & s:
                a, b = v[i], v[i + s]
                nxt[i] = jnp.where(low, a, pltpu.roll(b, s, axis=0))
                nxt[i + s] = jnp.where(low, pltpu.roll(a, SUBLANES - s, axis=0), b)
        v = nxt
        s //= 2
    return v


def _heads_to_rows(src_refs, dst_ref, rows):
    tp, ngroups = src_refs[0].shape[:2]
    assert rows % SUBLANES == 0

    def body(t, carry):
        p0 = pl.multiple_of(t * rows, rows)
        for c, ref in enumerate(src_refs):
            for hh in range(ngroups):
                x = ref[pl.ds(p0, rows), hh]
                parts = [_transpose_sublanes([x[g + i] for i in range(SUBLANES)])
                         for g in range(0, rows, SUBLANES)]
                for j in range(SUBLANES):
                    tile = jnp.concatenate([part[j] for part in parts], axis=0).astype(dst_ref.dtype)
                    dst_ref[hh * SUBLANES + j, pl.ds(p0, rows), c * LANES:(c + 1) * LANES] = tile
        return carry

    lax.fori_loop(0, tp // rows, body, 0)


def _attn_sample_kernel(lq1, lk1, lq2, lk2, subln_ref, q_ref, kn_ref, vn_ref, k0_ref, k1_ref, v0_ref, v1_ref,
                        o_ref, m_ref, l_ref, acc_ref, kt_ref, vt_ref, s_ref, p_ref,
                        *, past_len, nheads, lambda_init):
    d = HEAD_DIM
    tq = q_ref.shape[0]
    tp = k0_ref.shape[0]
    pc = pl.program_id(1)
    nt = (((1,), (1,)), ((), ()))

    @pl.when(pc == 0)
    def _():
        m_ref[...] = jnp.full_like(m_ref, NEG_INF)
        l_ref[...] = jnp.zeros_like(l_ref)
        acc_ref[...] = jnp.zeros_like(acc_ref)

    def stacked_q(h):
        q = q_ref[:, h * 2 * d:(h + 1) * 2 * d]
        z = jnp.zeros((tq, d), q.dtype)
        return jnp.concatenate([jnp.concatenate([q[:, :d], z], axis=1),
                                jnp.concatenate([z, q[:, d:]], axis=1)], axis=0)

    def qchunk(shape):
        return (past_len + lax.broadcasted_iota(jnp.int32, shape, 0) % tq) // CHUNK

    def softmax_step(s, valid):
        m_prev = m_ref[...]
        m_new = jnp.maximum(m_prev, jnp.max(s, axis=-1, keepdims=True))
        alpha = jnp.exp2(m_prev - m_new)
        p = jnp.exp2(s - m_new)
        if valid is not None:
            p = jnp.where(valid, p, 0.0)
        l_ref[...] = alpha * l_ref[...] + jnp.sum(p, axis=-1, keepdims=True)
        m_ref[...] = m_new
        acc_ref[...] = alpha * acc_ref[...]
        return p.astype(BF16)

    _heads_to_rows((k0_ref, k1_ref), kt_ref, BF16_ROWS)
    _heads_to_rows((v0_ref, v1_ref), vt_ref, BF16_ROWS)

    rows = 2 * tq
    for h in range(nheads):
        s_ref[h * rows:(h + 1) * rows, :] = lax.dot_general(stacked_q(h), kt_ref[h], nt,
                                                            preferred_element_type=F32)
    shape = (nheads * rows, tp)
    vis = (pc * tp + lax.broadcasted_iota(jnp.int32, shape, 1)) // CHUNK <= qchunk(shape)
    p_ref[...] = softmax_step(jnp.where(vis, s_ref[...], NEG_INF), None)
    for h in range(nheads):
        sl = slice(h * rows, (h + 1) * rows)
        acc_ref[sl, :] += jnp.dot(p_ref[sl, :], vt_ref[h], preferred_element_type=F32)

    @pl.when(pc == pl.num_programs(1) - 1)
    def _():
        lam = _lambda(lq1, lk1, lq2, lk2, lambda_init)
        pad = jnp.zeros((LANES - tq, 2 * d), kn_ref.dtype)
        vn = []
        for h in range(nheads):
            hs = slice(h * 2 * d, (h + 1) * 2 * d)
            kn = jnp.concatenate([kn_ref[:, hs], pad], axis=0)
            vn.append(jnp.concatenate([vn_ref[:, hs], pad], axis=0))
            s_ref[h * rows:(h + 1) * rows, :LANES] = lax.dot_general(stacked_q(h), kn, nt,
                                                                     preferred_element_type=F32)
        shape_n = (nheads * rows, LANES)
        col = lax.broadcasted_iota(jnp.int32, shape_n, 1)
        valid = col < tq
        vis_n = valid & ((past_len + col) // CHUNK <= qchunk(shape_n))
        p_n = softmax_step(jnp.where(vis_n, s_ref[:, :LANES], NEG_INF), valid)
        for h in range(nheads):
            sl = slice(h * rows, (h + 1) * rows)
            o2 = (acc_ref[sl, :] + jnp.dot(p_n[sl, :], vn[h], preferred_element_type=F32)) / l_ref[sl, :]
            o = o2[:tq] - lam * o2[tq:]
            o_ref[:, h * 2 * d:(h + 1) * 2 * d] = _subln(o, subln_ref, lambda_init).astype(o_ref.dtype)


def _attn_sample(q, kn, vn, cache_k, cache_v, lam_vecs, subln, lambda_init, tq):
    mt, width = q.shape
    nb, past_len, nh, hd2 = cache_k.shape
    tp = min(SAMPLE_CACHE_BLOCK, past_len)
    assert nh % SUBLANES == 0 and tp % SUBLANES == 0 and past_len % tp == 0
    cache_k = cache_k.reshape(nb, past_len, nh // SUBLANES, SUBLANES, hd2)
    cache_v = cache_v.reshape(nb, past_len, nh // SUBLANES, SUBLANES, hd2)
    vec = pl.BlockSpec((1, HEAD_DIM), lambda b, p: (0, 0))
    row = pl.BlockSpec((tq, width), lambda b, p: (b, 0))
    half = [pl.BlockSpec((None, tp, nh // SUBLANES, SUBLANES, HEAD_DIM),
                         functools.partial(lambda b, p, c: (b, p, 0, 0, c), c=c)) for c in range(2)]
    return pl.pallas_call(
        functools.partial(_attn_sample_kernel, past_len=past_len, nheads=nh, lambda_init=lambda_init),
        grid=(nb, past_len // tp),
        in_specs=[vec, vec, vec, vec, pl.BlockSpec((1, hd2), lambda b, p: (0, 0)),
                  row, row, row, half[0], half[1], half[0], half[1]],
        out_specs=row,
        out_shape=jax.ShapeDtypeStruct((mt, width), F32),
        scratch_shapes=[pltpu.VMEM((nh * 2 * tq, 1), F32), pltpu.VMEM((nh * 2 * tq, 1), F32),
                        pltpu.VMEM((nh * 2 * tq, hd2), F32),
                        pltpu.VMEM((nh, tp, hd2), BF16), pltpu.VMEM((nh, tp, hd2), BF16),
                        pltpu.VMEM((nh * 2 * tq, tp), F32), pltpu.VMEM((nh * 2 * tq, tp), BF16)],
        compiler_params=_params("parallel", "arbitrary"),
        name="attn_sample",
    )(*lam_vecs, subln, q, kn, vn, cache_k, cache_k, cache_v, cache_v)


def _sgu_merge_kernel(vs_ref, u_ref, ga_ref, gb_ref, oa_ref, gn_ref, w_ref, bt_ref, *outs,
                      seg, groups, want_v):
    vs = vs_ref[...]
    ms = jnp.mean(vs * vs, axis=-1, keepdims=True)
    vn = vs * lax.rsqrt(ms + EPS) * gn_ref[...]
    if want_v:
        outs[1][...] = vn
    rows = vs.shape[0]
    gd = vs.shape[1] // groups
    r = lax.broadcasted_iota(jnp.int32, (rows, rows), 0)
    c = lax.broadcasted_iota(jnp.int32, (rows, rows), 1)
    keep = (r // seg == c // seg) & (c <= r)
    vb = vn.astype(BF16)
    for g in range(groups):
        sl = slice(g * gd, (g + 1) * gd)
        w = jnp.where(keep, w_ref[g], 0.0).astype(BF16)
        s = jnp.dot(w, vb[:, sl], preferred_element_type=F32) + bt_ref[:, g:g + 1]
        o_sgu = u_ref[:, sl] * s
        outs[0][:, sl] = (ga_ref[:, sl] * oa_ref[:, sl] + gb_ref[:, sl] * o_sgu).astype(outs[0].dtype)


def _sgu_merge(vs, u, gate, o_attn, sgu_norm, w_s, b_s, seg, want_v):
    m, d = vs.shape
    rows = SGU_CHUNK
    groups = w_s.shape[0]
    rep = rows // seg
    w_t = jnp.tile(w_s[:, :seg, :seg], (1, rep, rep))
    b_t = jnp.tile(jnp.swapaxes(b_s[:, :seg], 0, 1), (rep, 1))
    blk = pl.BlockSpec((rows, d), lambda i: (i, 0))
    out_shape = [jax.ShapeDtypeStruct((m, d), BF16)]
    if want_v:
        out_shape.append(jax.ShapeDtypeStruct((m, d), F32))
    return pl.pallas_call(
        functools.partial(_sgu_merge_kernel, seg=seg, groups=groups, want_v=want_v),
        grid=(m // rows,),
        in_specs=[blk, blk, blk, pl.BlockSpec((rows, d), lambda i: (i, 1)), blk,
                  pl.BlockSpec((1, d), lambda i: (0, 0)),
                  pl.BlockSpec((groups, rows, rows), lambda i: (0, 0, 0)),
                  pl.BlockSpec((rows, groups), lambda i: (0, 0))],
        out_specs=[blk] * len(out_shape),
        out_shape=out_shape,
        compiler_params=_params("parallel"),
        name="sgu_merge",
    )(vs, u, gate, gate, o_attn, sgu_norm.reshape(1, d), w_t, b_t)


def _layer(x, pos, attend, seg, want_v, tiles, lambda_init, wts, w16):
    (norm_mix, w_in, b_gate, q_norm, k_norm, subln, sgu_norm, w_s, b_s, w_o, norm_ffn, w_up, w_down) = wts
    m, d = x.shape
    qk_w = N_HEADS * 2 * HEAD_DIM
    (tm, tn), (tm_down, tn_down) = tiles
    tn, tn_down = min(tn, d), min(tn_down, d)
    rounding = not w16
    cos, sin = _rope_tables(pos)

    def mm(name, a, w, col0, ncols, epilogue, extras, outs, ride=None):
        if not rounding:
            return _matmul(a, w16[name], 0, ncols, tm, tn, epilogue, extras, outs, name)
        res = _matmul_cast(a, w, col0, ncols, tm, tn, epilogue, extras, outs, name,
                           None if ride is None else ride[1])
        w16[name] = res[len(outs)]
        if ride is not None:
            w16[ride[0]] = res[len(outs) + 1]
        return res[:len(outs)]

    def rope_extras(g, rows):
        npos = pos.shape[0] // rows
        return [(g.reshape(1, HEAD_DIM), (1, HEAD_DIM), lambda n, i, k: (0, 0)),
                (cos, (rows, HEAD_DIM), lambda n, i, k: (i % npos, 0)),
                (sin, (rows, HEAD_DIM), lambda n, i, k: (i % npos, 0))]

    def res_extras(a, rows, cols):
        return [(a, (rows, cols), lambda n, i, k: (i, n))]

    h = _rmsnorm(x, norm_mix, min(tm, 256))
    (q,) = mm("proj_q", h, w_in, 0, qk_w, functools.partial(_epi_norm_rope, scale=HEAD_DIM ** -0.5 * LOG2E),
              rope_extras(q_norm, tm), [BF16])
    k32, kb = mm("proj_k", h, w_in, qk_w, qk_w, functools.partial(_epi_norm_rope, scale=1.0),
                 rope_extras(k_norm, tm), [F32, BF16])
    v32, vb = mm("proj_v", h, w_in, 2 * qk_w, qk_w, _epi_plain, [], [F32, BF16])
    (u,) = mm("proj_u", h, w_in, 3 * qk_w, d, _epi_plain, [], [F32])
    (vs,) = mm("proj_vs", h, w_in, 3 * qk_w + d, d, _epi_plain, [], [F32])
    (gate,) = mm("proj_gate", h, w_in, 3 * qk_w + 2 * d, 2 * d, _epi_gate,
                 [(b_gate.reshape(1, 2 * d), (1, tn), lambda n, i, k: (0, n))], [F32])

    o_attn = attend(q, kb, vb)
    res = _sgu_merge(vs, u, gate, o_attn, sgu_norm, w_s, b_s, seg, want_v)
    merged = res[0]
    (x1,) = mm("proj_out", merged, w_o, 0, d, _epi_residual, res_extras(x, tm, tn), [F32])
    hf = _rmsnorm(x1, norm_ffn, min(tm, 256))
    (hid,) = mm("ffn_up", hf, w_up, 0, w_up.shape[1], _epi_relu2, [], [BF16], ride=("ffn_down", w_down))
    (y,) = _matmul(hid, w16["ffn_down"], 0, d, tm_down, tn_down, _epi_residual,
                   res_extras(x1, tm_down, tn_down), [F32], "ffn_down")
    return y, k32, v32, (res[1] if want_v else None)


def kernel(x_prompt, x_sample, cache_k_attn, cache_v_attn, norm_mix, w_in, b_gate, q_norm, k_norm, lambda_q1, lambda_k1, lambda_q2, lambda_k2, subln, sgu_norm, w_s, b_s, w_o, norm_ffn, w_up, w_down):
    depth = w_in.shape[0]
    bp, tp, d = x_prompt.shape
    bs, ts, _ = x_sample.shape
    hd2 = 2 * HEAD_DIM
    pos_p = jnp.arange(tp, dtype=jnp.int32)
    pos_s = jnp.tile(PAST_LEN + jnp.arange(ts, dtype=jnp.int32), bs)
    xp = x_prompt.reshape(bp * tp, d)
    xs = x_sample.reshape(bs * ts, d)
    kp_l, vp_l, ks_l, vs_l, sg_l = [], [], [], [], []
    for l in range(depth):
        lambda_init = 0.8 - 0.6 * math.exp(-0.3 * l)
        lam_vecs = [a[l].reshape(1, HEAD_DIM) for a in (lambda_q1, lambda_k1, lambda_q2, lambda_k2)]
        sub = subln[l].reshape(1, hd2)
        wts = (norm_mix[l], w_in[l], b_gate[l], q_norm[l], k_norm[l], sub, sgu_norm[l],
               w_s[l], b_s[l], w_o[l], norm_ffn[l], w_up[l], w_down[l])
        ck, cv = cache_k_attn[l], cache_v_attn[l]

        attend_p = lambda q, k, v: _attn_prompt(q, k, v, lam_vecs, sub, lambda_init, bp, min(ATTN_BLOCK, tp))
        attend_s = lambda q, k, v: _attn_sample(q, k, v, ck, cv, lam_vecs, sub, lambda_init, ts)
        w16 = {}
        xp, kp, vp, _ = _layer(xp, pos_p, attend_p, min(tp, SGU_CHUNK), False, PROMPT_TILES, lambda_init, wts, w16)
        decode_tiles = ((bs * ts, DECODE_TN), (bs * ts, DECODE_TN))
        xs, ksm, vsm, sgv = _layer(xs, pos_s, attend_s, min(ts, SGU_CHUNK), True, decode_tiles, lambda_init, wts, w16)
        kp_l.append(kp.reshape(bp, tp, N_HEADS, hd2))
        vp_l.append(vp.reshape(bp, tp, N_HEADS, hd2))
        ks_l.append(ksm.reshape(bs, ts, N_HEADS, hd2))
        vs_l.append(vsm.reshape(bs, ts, N_HEADS, hd2))
        sg_l.append(sgv.reshape(bs, ts, d))
    return (xp.reshape(bp, tp, d), xs.reshape(bs, ts, d), jnp.stack(kp_l), jnp.stack(vp_l),
            jnp.stack(ks_l), jnp.stack(vs_l), jnp.stack(sg_l))
```

```python
import functools
import math

import jax
import jax.numpy as jnp
from jax import lax
from jax.experimental import pallas as pl
from jax.experimental.pallas import tpu as pltpu

D_MODEL = 4096
BATCH = 2
SEQ = 4096
DEC_BATCH = 16
DEC_SEQ = 16
PAST_LEN = 4096
CHUNK = 64
N_HEADS = 16
HEAD_DIM = 128
SGU_CHUNK = 128
SGU_GROUPS = 8
D_FF = 4 * D_MODEL
ROPE_THETA = 10000.0
EPS = 1e-6
NEG_INF = -1e30

LANES = 128
SUBLANES = 8
BF16_ROWS = 16
MXU_WIDTH = 256
EPILOGUE_ROWS = 256
VMEM_LIMIT_BYTES = 60 * 1024 * 1024
MAX_TK = 4096
SAMPLE_CACHE_BLOCK = 512
ATTN_BLOCK = 512
PROMPT_TILES = ((512, 1024), (512, 1024))
WIDE_ROW_FACTOR = 2
DECODE_TN = 1024
LOG2E = math.log2(math.e)
F32 = jnp.float32
BF16 = jnp.bfloat16


def _params(*sem):
    return pltpu.CompilerParams(dimension_semantics=sem, vmem_limit_bytes=VMEM_LIMIT_BYTES)


def _rmsnorm_kernel(x_ref, g_ref, o_ref):
    x = x_ref[...]
    ms = jnp.mean(x * x, axis=-1, keepdims=True)
    o_ref[...] = (x * lax.rsqrt(ms + EPS) * g_ref[...]).astype(o_ref.dtype)


def _rmsnorm(x, g, tm):
    m, d = x.shape
    return pl.pallas_call(
        _rmsnorm_kernel,
        grid=(m // tm,),
        in_specs=[pl.BlockSpec((tm, d), lambda i: (i, 0)), pl.BlockSpec((1, d), lambda i: (0, 0))],
        out_specs=pl.BlockSpec((tm, d), lambda i: (i, 0)),
        out_shape=jax.ShapeDtypeStruct((m, d), BF16),
        compiler_params=_params("parallel"),
        name="rmsnorm",
    )(x, g.reshape(1, d))


def _mm_kernel(x_ref, w_ref, *refs, nk, n_extra, n_out, epilogue):
    extras, outs = refs[:n_extra], refs[n_extra:n_extra + n_out]
    if nk == 1:
        acc = jnp.dot(x_ref[...], w_ref[...], preferred_element_type=F32)
        epilogue(acc, *extras, *outs)
        return
    acc_ref = refs[n_extra + n_out]
    k = pl.program_id(2)

    @pl.when(k == 0)
    def _():
        acc_ref[...] = jnp.zeros_like(acc_ref)

    acc_ref[...] += jnp.dot(x_ref[...], w_ref[...], preferred_element_type=F32)

    @pl.when(k == nk - 1)
    def _():
        epilogue(acc_ref[...], *extras, *outs)


def _matmul(x, w, col0, ncols, tm, tn, epilogue, extras, outs, name):
    m, kdim = x.shape
    tk = min(kdim, MAX_TK)
    nk = kdim // tk
    assert col0 % tn == 0 and ncols % tn == 0 and m % tm == 0 and kdim % tk == 0
    cb = col0 // tn
    in_specs = [pl.BlockSpec((tm, tk), lambda n, i, k: (i, k)),
                pl.BlockSpec((tk, tn), lambda n, i, k: (k, n + cb))]
    in_specs += [pl.BlockSpec(bs, im) for _, bs, im in extras]
    out_specs = [pl.BlockSpec((tm, tn), lambda n, i, k: (i, n)) for _ in outs]
    out_shape = [jax.ShapeDtypeStruct((m, ncols), dt) for dt in outs]
    scratch = [pltpu.VMEM((tm, tn), F32)] if nk > 1 else []
    res = pl.pallas_call(
        functools.partial(_mm_kernel, nk=nk, n_extra=len(extras), n_out=len(outs), epilogue=epilogue),
        grid=(ncols // tn, m // tm, nk),
        in_specs=in_specs,
        out_specs=out_specs,
        out_shape=out_shape,
        scratch_shapes=scratch,
        compiler_params=_params("parallel", "parallel", "arbitrary"),
        name=name,
    )(x, w, *[a for a, _, _ in extras])
    return res


def _mm_cast_kernel(x_ref, w_ref, *refs, nn, n_extra, n_out, epilogue, ride):
    extras = refs[:n_extra]
    pos = n_extra + (1 if ride else 0)
    outs, w16_ref = refs[pos:pos + n_out], refs[pos + n_out]
    wb_ref = refs[-1]
    n, i = pl.program_id(0), pl.program_id(1)
    chunk_rows = w_ref.shape[0]

    @pl.when(n < nn)
    def _():
        chunk = w_ref[...].astype(BF16)
        wb_ref[n % 2, pl.ds(pl.multiple_of(i * chunk_rows, chunk_rows), chunk_rows), :] = chunk
        w16_ref[...] = chunk

    @pl.when(n > 0)
    def _():
        if ride:
            refs[pos + n_out + 1][...] = refs[n_extra][...].astype(BF16)
        tm = x_ref.shape[0]
        rows = min(EPILOGUE_ROWS, tm)
        w = wb_ref[(n - 1) % 2]
        for r0 in range(0, tm, rows):
            part = [r.at[r0:r0 + rows, :] if r.shape[0] == tm else r for r in (*extras, *outs)]
            acc = jnp.dot(x_ref[r0:r0 + rows, :], w, preferred_element_type=F32)
            epilogue(acc, *part)


def _matmul_cast(x, w, col0, ncols, tm, tn, epilogue, extras, outs, name, ride=None):
    m, kdim = x.shape
    nn, nm = ncols // tn, m // tm
    assert col0 % tn == 0 and ncols % tn == 0 and m % tm == 0 and kdim <= MAX_TK
    assert kdim % (nm * BF16_ROWS) == 0
    cb = col0 // tn
    chunk = kdim // nm

    def row(n, i):
        return jnp.where(n == 0, 0, i)

    def col(n):
        return jnp.maximum(n - 1, 0)

    def loading(n, i):
        return jnp.where(n == nn, nm - 1, i), jnp.minimum(n, nn - 1)

    in_specs = [pl.BlockSpec((tm, kdim), lambda n, i, k: (row(n, i), 0)),
                pl.BlockSpec((chunk, tn), lambda n, i, k: (loading(n, i)[0], loading(n, i)[1] + cb))]
    in_specs += [pl.BlockSpec(bs, functools.partial(lambda n, i, k, im: im(col(n), row(n, i), k), im=im))
                 for _, bs, im in extras]
    out_specs = [pl.BlockSpec((tm, tn), lambda n, i, k: (row(n, i), col(n))) for _ in outs]
    out_specs.append(pl.BlockSpec((chunk, tn), lambda n, i, k: loading(n, i)))
    out_shape = [jax.ShapeDtypeStruct((m, ncols), dt) for dt in outs]
    out_shape.append(jax.ShapeDtypeStruct((kdim, ncols), BF16))
    operands = [x, w] + [a for a, _, _ in extras]
    if ride is not None:
        rrows, rcols = ride.shape
        assert rrows % (nn * nm * BF16_ROWS) == 0
        step_rows = rrows // (nn * nm)
        ride_spec = pl.BlockSpec((step_rows, rcols), lambda n, i, k: (col(n) * nm + row(n, i), 0))
        in_specs.append(ride_spec)
        out_specs.append(ride_spec)
        out_shape.append(jax.ShapeDtypeStruct(ride.shape, BF16))
        operands.append(ride)
    return pl.pallas_call(
        functools.partial(_mm_cast_kernel, nn=nn, n_extra=len(extras), n_out=len(outs), epilogue=epilogue,
                          ride=ride is not None),
        grid=(nn + 1, nm, 1),
        in_specs=in_specs,
        out_specs=out_specs,
        out_shape=out_shape,
        scratch_shapes=[pltpu.VMEM((2, kdim, tn), BF16)],
        compiler_params=_params("arbitrary", "arbitrary", "arbitrary"),
        name=name,
    )(*operands)


def _epi_plain(acc, *outs):
    for o in outs:
        o[...] = acc.astype(o.dtype)


def _epi_norm_rope(acc, g_ref, cos_ref, sin_ref, *outs, scale):
    g = g_ref[...]
    cos = cos_ref[...]
    sin = sin_ref[...]
    width = min(MXU_WIDTH, acc.shape[1])
    row = lax.broadcasted_iota(jnp.int32, (width, width), 0) // HEAD_DIM
    col = lax.broadcasted_iota(jnp.int32, (width, width), 1) // HEAD_DIM
    same_head = jnp.where(row == col, 1.0, 0.0).astype(BF16)
    for c0 in range(0, acc.shape[1], width):
        z = acc[:, c0:c0 + width]
        ss = jnp.dot((z * z).astype(BF16), same_head, preferred_element_type=F32)
        normed = z * lax.rsqrt(ss * (1.0 / HEAD_DIM) + EPS)
        for j in range(0, width, HEAD_DIM):
            y = normed[:, j:j + HEAD_DIM] * g
            y = y * cos + pltpu.roll(y, HEAD_DIM // 2, axis=1) * sin
            if scale != 1.0:
                y = y * scale
            for o in outs:
                o[:, c0 + j:c0 + j + HEAD_DIM] = y.astype(o.dtype)


def _epi_gate(acc, b_ref, o_ref):
    o_ref[...] = jax.nn.sigmoid(acc + b_ref[...]).astype(o_ref.dtype)


def _epi_residual(acc, r_ref, o_ref):
    o_ref[...] = (r_ref[...] + acc).astype(o_ref.dtype)


def _epi_relu2(acc, o_ref):
    o_ref[...] = jnp.square(jnp.maximum(acc, 0.0)).astype(o_ref.dtype)


def _rope_tables(pos):
    half = HEAD_DIM // 2
    inv = ROPE_THETA ** (-jnp.arange(half, dtype=F32) / half)
    ang = pos.astype(F32)[:, None] * inv[None, :]
    cos, sin = jnp.cos(ang), jnp.sin(ang)
    return jnp.concatenate([cos, cos], axis=-1), jnp.concatenate([-sin, sin], axis=-1)


def _lambda(lq1, lk1, lq2, lk2, lambda_init):
    s1 = jnp.sum(lq1[...] * lk1[...], axis=-1, keepdims=True)
    s2 = jnp.sum(lq2[...] * lk2[...], axis=-1, keepdims=True)
    return jnp.exp(s1) - jnp.exp(s2) + lambda_init


def _subln(o, subln_ref, lambda_init):
    ms = jnp.mean(o * o, axis=-1, keepdims=True)
    return o * lax.rsqrt(ms + EPS) * subln_ref[...] * (1.0 - lambda_init)


def _for_each_block(count, fn):
    def pair(t, carry):
        fn(2 * t, 2 * t + 1)
        return carry

    lax.fori_loop(0, count // 2, pair, 0)

    @pl.when(count % 2 == 1)
    def _():
        fn(count - 1)


def _attn_prompt_kernel(lq1, lk1, lq2, lk2, subln_ref, bias_ref, q_ref, k_ref, v_ref, o_ref,
                        s_ref, acc_ref, mp_ref, mb_ref, lp_ref, *, tq, lambda_init):
    qi = pl.program_id(2)
    d = HEAD_DIM
    nl = tq // LANES
    nt = (((1,), (1,)), ((), ()))
    mp_ref[...] = jnp.full_like(mp_ref, NEG_INF)

    def scores(blocks, masked):
        for c in range(2):
            m = mp_ref[c]
            for j in blocks:
                off = pl.multiple_of(j * tq, tq)
                s = lax.dot_general(q_ref[:, c * d:(c + 1) * d], k_ref[pl.ds(off, tq), c * d:(c + 1) * d], nt,
                                    preferred_element_type=F32)
                if masked:
                    s = s + bias_ref[...]
                s_ref[c, j] = s
                for t in range(nl):
                    m = jnp.maximum(m, s[:, t * LANES:(t + 1) * LANES])
            mp_ref[c] = m

    _for_each_block(qi, lambda *blocks: scores(blocks, False))
    scores((qi,), True)

    for c in range(2):
        mb_ref[c] = jnp.broadcast_to(jnp.max(mp_ref[c], axis=-1, keepdims=True), (tq, LANES))
    lp_ref[...] = jnp.zeros_like(lp_ref)
    acc_ref[...] = jnp.zeros_like(acc_ref)

    def accumulate(*blocks):
        for c in range(2):
            mb = mb_ref[c]
            lp = lp_ref[c]
            for j in blocks:
                s = s_ref[c, j]
                ps = [jnp.exp2(s[:, t * LANES:(t + 1) * LANES] - mb) for t in range(nl)]
                for p in ps:
                    lp = lp + p
                pb = jnp.concatenate(ps, axis=1).astype(BF16)
                acc_ref[c] += jnp.dot(pb, v_ref[pl.ds(pl.multiple_of(j * tq, tq), tq), :],
                                      preferred_element_type=F32)
            lp_ref[c] = lp

    _for_each_block(qi + 1, accumulate)

    lam = _lambda(lq1, lk1, lq2, lk2, lambda_init)
    l0 = jnp.sum(lp_ref[0], axis=-1, keepdims=True)
    l1 = jnp.sum(lp_ref[1], axis=-1, keepdims=True)
    o = acc_ref[0] / l0 - lam * (acc_ref[1] / l1)
    o_ref[...] = _subln(o, subln_ref, lambda_init).astype(o_ref.dtype)


def _attn_prompt(q, k, v, lam_vecs, subln, lambda_init, batch, tq):
    mt, width = q.shape
    t = mt // batch
    hd2 = 2 * HEAD_DIM
    nh = width // hd2
    nq = t // tq
    chunk_of = jnp.arange(tq, dtype=jnp.int32) // CHUNK
    bias = jnp.where(chunk_of[None, :] <= chunk_of[:, None], 0.0, NEG_INF).astype(F32)
    vec = pl.BlockSpec((1, HEAD_DIM), lambda b, h, i: (0, 0))
    return pl.pallas_call(
        functools.partial(_attn_prompt_kernel, tq=tq, lambda_init=lambda_init),
        grid=(batch, nh, nq),
        in_specs=[vec, vec, vec, vec,
                  pl.BlockSpec((1, hd2), lambda b, h, i: (0, 0)),
                  pl.BlockSpec((tq, tq), lambda b, h, i: (0, 0)),
                  pl.BlockSpec((tq, hd2), lambda b, h, i: (b * nq + i, h)),
                  pl.BlockSpec((t, hd2), lambda b, h, i: (b, h)),
                  pl.BlockSpec((t, hd2), lambda b, h, i: (b, h))],
        out_specs=pl.BlockSpec((tq, hd2), lambda b, h, i: (b * nq + i, h)),
        out_shape=jax.ShapeDtypeStruct((mt, width), BF16),
        scratch_shapes=[pltpu.VMEM((2, nq, tq, tq), F32), pltpu.VMEM((2, tq, hd2), F32),
                        pltpu.VMEM((2, tq, LANES), F32), pltpu.VMEM((2, tq, LANES), F32),
                        pltpu.VMEM((2, tq, LANES), F32)],
        compiler_params=_params("parallel", "parallel", "arbitrary"),
        name="attn_prompt",
    )(*lam_vecs, subln, bias, q, k, v)


def _transpose_sublanes(x):
    v = list(x)
    sub = lax.broadcasted_iota(jnp.int32, (SUBLANES, LANES), 0)
    s = SUBLANES // 2
    while s:
        low = (sub & s) == 0
        nxt = list(v)
        for i in range(SUBLANES):
            if not i & s:
                a, b = v[i], v[i + s]
                nxt[i] = jnp.where(low, a, pltpu.roll(b, s, axis=0))
                nxt[i + s] = jnp.where(low, pltpu.roll(a, SUBLANES - s, axis=0), b)
        v = nxt
        s //= 2
    return v


def _heads_to_rows(src_refs, dst_ref, rows):
    tp, ngroups = src_refs[0].shape[:2]
    assert rows % SUBLANES == 0

    def body(t, carry):
        p0 = pl.multiple_of(t * rows, rows)
        for c, ref in enumerate(src_refs):
            for hh in range(ngroups):
                x = ref[pl.ds(p0, rows), hh]
                parts = [_transpose_sublanes([x[g + i] for i in range(SUBLANES)])
                         for g in range(0, rows, SUBLANES)]
                for j in range(SUBLANES):
                    tile = jnp.concatenate([part[j] for part in parts], axis=0).astype(dst_ref.dtype)
                    dst_ref[hh * SUBLANES + j, pl.ds(p0, rows), c * LANES:(c + 1) * LANES] = tile
        return carry

    lax.fori_loop(0, tp // rows, body, 0)


def _attn_sample_kernel(lq1, lk1, lq2, lk2, subln_ref, q_ref, kn_ref, vn_ref, k0_ref, k1_ref, v0_ref, v1_ref,
                        o_ref, m_ref, l_ref, acc_ref, kt_ref, vt_ref, s_ref, p_ref,
                        *, past_len, nheads, lambda_init):
    d = HEAD_DIM
    tq = q_ref.shape[0]
    tp = k0_ref.shape[0]
    pc = pl.program_id(1)
    nt = (((1,), (1,)), ((), ()))

    @pl.when(pc == 0)
    def _():
        m_ref[...] = jnp.full_like(m_ref, NEG_INF)
        l_ref[...] = jnp.zeros_like(l_ref)
        acc_ref[...] = jnp.zeros_like(acc_ref)

    def stacked_q(h):
        q = q_ref[:, h * 2 * d:(h + 1) * 2 * d]
        z = jnp.zeros((tq, d), q.dtype)
        return jnp.concatenate([jnp.concatenate([q[:, :d], z], axis=1),
                                jnp.concatenate([z, q[:, d:]], axis=1)], axis=0)

    def qchunk(shape):
        return (past_len + lax.broadcasted_iota(jnp.int32, shape, 0) % tq) // CHUNK

    def softmax_step(s, valid):
        m_prev = m_ref[...]
        m_new = jnp.maximum(m_prev, jnp.max(s, axis=-1, keepdims=True))
        alpha = jnp.exp2(m_prev - m_new)
        p = jnp.exp2(s - m_new)
        if valid is not None:
            p = jnp.where(valid, p, 0.0)
        l_ref[...] = alpha * l_ref[...] + jnp.sum(p, axis=-1, keepdims=True)
        m_ref[...] = m_new
        acc_ref[...] = alpha * acc_ref[...]
        return p.astype(BF16)

    _heads_to_rows((k0_ref, k1_ref), kt_ref, BF16_ROWS)
    _heads_to_rows((v0_ref, v1_ref), vt_ref, BF16_ROWS)

    rows = 2 * tq
    for h in range(nheads):
        s_ref[h * rows:(h + 1) * rows, :] = lax.dot_general(stacked_q(h), kt_ref[h], nt,
                                                            preferred_element_type=F32)
    shape = (nheads * rows, tp)
    vis = (pc * tp + lax.broadcasted_iota(jnp.int32, shape, 1)) // CHUNK <= qchunk(shape)
    p_ref[...] = softmax_step(jnp.where(vis, s_ref[...], NEG_INF), None)
    for h in range(nheads):
        sl = slice(h * rows, (h + 1) * rows)
        acc_ref[sl, :] += jnp.dot(p_ref[sl, :], vt_ref[h], preferred_element_type=F32)

    @pl.when(pc == pl.num_programs(1) - 1)
    def _():
        lam = _lambda(lq1, lk1, lq2, lk2, lambda_init)
        pad = jnp.zeros((LANES - tq, 2 * d), kn_ref.dtype)
        vn = []
        for h in range(nheads):
            hs = slice(h * 2 * d, (h + 1) * 2 * d)
            kn = jnp.concatenate([kn_ref[:, hs], pad], axis=0)
            vn.append(jnp.concatenate([vn_ref[:, hs], pad], axis=0))
            s_ref[h * rows:(h + 1) * rows, :LANES] = lax.dot_general(stacked_q(h), kn, nt,
                                                                     preferred_element_type=F32)
        shape_n = (nheads * rows, LANES)
        col = lax.broadcasted_iota(jnp.int32, shape_n, 1)
        valid = col < tq
        vis_n = valid & ((past_len + col) // CHUNK <= qchunk(shape_n))
        p_n = softmax_step(jnp.where(vis_n, s_ref[:, :LANES], NEG_INF), valid)
        for h in range(nheads):
            sl = slice(h * rows, (h + 1) * rows)
            o2 = (acc_ref[sl, :] + jnp.dot(p_n[sl, :], vn[h], preferred_element_type=F32)) / l_ref[sl, :]
            o = o2[:tq] - lam * o2[tq:]
            o_ref[:, h * 2 * d:(h + 1) * 2 * d] = _subln(o, subln_ref, lambda_init).astype(o_ref.dtype)


def _attn_sample(q, kn, vn, cache_k, cache_v, lam_vecs, subln, lambda_init, tq):
    mt, width = q.shape
    nb, past_len, nh, hd2 = cache_k.shape
    tp = min(SAMPLE_CACHE_BLOCK, past_len)
    assert nh % SUBLANES == 0 and tp % SUBLANES == 0 and past_len % tp == 0
    cache_k = cache_k.reshape(nb, past_len, nh // SUBLANES, SUBLANES, hd2)
    cache_v = cache_v.reshape(nb, past_len, nh // SUBLANES, SUBLANES, hd2)
    vec = pl.BlockSpec((1, HEAD_DIM), lambda b, p: (0, 0))
    row = pl.BlockSpec((tq, width), lambda b, p: (b, 0))
    half = [pl.BlockSpec((None, tp, nh // SUBLANES, SUBLANES, HEAD_DIM),
                         functools.partial(lambda b, p, c: (b, p, 0, 0, c), c=c)) for c in range(2)]
    return pl.pallas_call(
        functools.partial(_attn_sample_kernel, past_len=past_len, nheads=nh, lambda_init=lambda_init),
        grid=(nb, past_len // tp),
        in_specs=[vec, vec, vec, vec, pl.BlockSpec((1, hd2), lambda b, p: (0, 0)),
                  row, row, row, half[0], half[1], half[0], half[1]],
        out_specs=row,
        out_shape=jax.ShapeDtypeStruct((mt, width), BF16),
        scratch_shapes=[pltpu.VMEM((nh * 2 * tq, 1), F32), pltpu.VMEM((nh * 2 * tq, 1), F32),
                        pltpu.VMEM((nh * 2 * tq, hd2), F32),
                        pltpu.VMEM((nh, tp, hd2), BF16), pltpu.VMEM((nh, tp, hd2), BF16),
                        pltpu.VMEM((nh * 2 * tq, tp), F32), pltpu.VMEM((nh * 2 * tq, tp), BF16)],
        compiler_params=_params("parallel", "arbitrary"),
        name="attn_sample",
    )(*lam_vecs, subln, q, kn, vn, cache_k, cache_k, cache_v, cache_v)


def _sgu_merge_kernel(vs_ref, u_ref, ga_ref, gb_ref, oa_ref, gn_ref, w_ref, bt_ref, *outs,
                      seg, groups, want_v):
    vs = vs_ref[...]
    ms = jnp.mean(vs * vs, axis=-1, keepdims=True)
    vn = vs * lax.rsqrt(ms + EPS) * gn_ref[...]
    if want_v:
        outs[1][...] = vn
    rows = vs.shape[0]
    gd = vs.shape[1] // groups
    r = lax.broadcasted_iota(jnp.int32, (rows, rows), 0)
    c = lax.broadcasted_iota(jnp.int32, (rows, rows), 1)
    keep = (r // seg == c // seg) & (c <= r)
    vb = vn.astype(BF16)
    for g in range(groups):
        sl = slice(g * gd, (g + 1) * gd)
        w = jnp.where(keep, w_ref[g], 0.0).astype(BF16)
        s = jnp.dot(w, vb[:, sl], preferred_element_type=F32) + bt_ref[:, g:g + 1]
        o_sgu = u_ref[:, sl].astype(F32) * s
        merged = (ga_ref[:, sl].astype(F32) * oa_ref[:, sl].astype(F32)
                  + gb_ref[:, sl].astype(F32) * o_sgu)
        outs[0][:, sl] = merged.astype(outs[0].dtype)


def _sgu_merge(vs, u, gate, o_attn, sgu_norm, w_s, b_s, seg, want_v):
    m, d = vs.shape
    rows = SGU_CHUNK
    groups = w_s.shape[0]
    rep = rows // seg
    w_t = jnp.tile(w_s[:, :seg, :seg], (1, rep, rep))
    b_t = jnp.tile(jnp.swapaxes(b_s[:, :seg], 0, 1), (rep, 1))
    blk = pl.BlockSpec((rows, d), lambda i: (i, 0))
    out_shape = [jax.ShapeDtypeStruct((m, d), BF16)]
    if want_v:
        out_shape.append(jax.ShapeDtypeStruct((m, d), F32))
    return pl.pallas_call(
        functools.partial(_sgu_merge_kernel, seg=seg, groups=groups, want_v=want_v),
        grid=(m // rows,),
        in_specs=[blk, blk, blk, pl.BlockSpec((rows, d), lambda i: (i, 1)), blk,
                  pl.BlockSpec((1, d), lambda i: (0, 0)),
                  pl.BlockSpec((groups, rows, rows), lambda i: (0, 0, 0)),
                  pl.BlockSpec((rows, groups), lambda i: (0, 0))],
        out_specs=[blk] * len(out_shape),
        out_shape=out_shape,
        compiler_params=_params("parallel"),
        name="sgu_merge",
    )(vs, u, gate, gate, o_attn, sgu_norm.reshape(1, d), w_t, b_t)


def _layer(x, pos, attend, seg, want_v, tiles, lambda_init, wts, w16):
    (norm_mix, w_in, b_gate, q_norm, k_norm, subln, sgu_norm, w_s, b_s, w_o, norm_ffn, w_up, w_down) = wts
    m, d = x.shape
    qk_w = N_HEADS * 2 * HEAD_DIM
    (tm, tn), (tm_down, tn_down) = tiles
    tn, tn_down = min(tn, d), min(tn_down, d)
    rounding = not w16
    cos, sin = _rope_tables(pos)

    def mm(name, a, w, col0, ncols, epilogue, extras, outs, ride=None, wide=False):
        rows = min(tm * WIDE_ROW_FACTOR, m) if wide else tm
        if not rounding:
            return _matmul(a, w16[name], 0, ncols, rows, tn, epilogue, extras, outs, name)
        res = _matmul_cast(a, w, col0, ncols, rows, tn, epilogue, extras, outs, name,
                           None if ride is None else ride[1])
        w16[name] = res[len(outs)]
        if ride is not None:
            w16[ride[0]] = res[len(outs) + 1]
        return res[:len(outs)]

    def rope_extras(g, rows):
        npos = pos.shape[0] // rows
        return [(g.reshape(1, HEAD_DIM), (1, HEAD_DIM), lambda n, i, k: (0, 0)),
                (cos, (rows, HEAD_DIM), lambda n, i, k: (i % npos, 0)),
                (sin, (rows, HEAD_DIM), lambda n, i, k: (i % npos, 0))]

    def res_extras(a, rows, cols):
        return [(a, (rows, cols), lambda n, i, k: (i, n))]

    h = _rmsnorm(x, norm_mix, min(tm, 256))
    (q,) = mm("proj_q", h, w_in, 0, qk_w, functools.partial(_epi_norm_rope, scale=HEAD_DIM ** -0.5 * LOG2E),
              rope_extras(q_norm, tm), [BF16])
    k32, kb = mm("proj_k", h, w_in, qk_w, qk_w, functools.partial(_epi_norm_rope, scale=1.0),
                 rope_extras(k_norm, tm), [F32, BF16])
    v32, vb = mm("proj_v", h, w_in, 2 * qk_w, qk_w, _epi_plain, [], [F32, BF16])
    (u,) = mm("proj_u", h, w_in, 3 * qk_w, d, _epi_plain, [], [BF16], wide=True)
    (vs,) = mm("proj_vs", h, w_in, 3 * qk_w + d, d, _epi_plain, [], [F32], wide=True)
    (gate,) = mm("proj_gate", h, w_in, 3 * qk_w + 2 * d, 2 * d, _epi_gate,
                 [(b_gate.reshape(1, 2 * d), (1, tn), lambda n, i, k: (0, n))], [BF16], wide=True)

    o_attn = attend(q, kb, vb)
    res = _sgu_merge(vs, u, gate, o_attn, sgu_norm, w_s, b_s, seg, want_v)
    merged = res[0]
    (x1,) = mm("proj_out", merged, w_o, 0, d, _epi_residual, res_extras(x, tm, tn), [F32])
    hf = _rmsnorm(x1, norm_ffn, min(tm, 256))
    (hid,) = mm("ffn_up", hf, w_up, 0, w_up.shape[1], _epi_relu2, [], [BF16], ride=("ffn_down", w_down),
                wide=True)
    (y,) = _matmul(hid, w16["ffn_down"], 0, d, tm_down, tn_down, _epi_residual,
                   res_extras(x1, tm_down, tn_down), [F32], "ffn_down")
    return y, k32, v32, (res[1] if want_v else None)


def kernel(x_prompt, x_sample, cache_k_attn, cache_v_attn, norm_mix, w_in, b_gate, q_norm, k_norm, lambda_q1, lambda_k1, lambda_q2, lambda_k2, subln, sgu_norm, w_s, b_s, w_o, norm_ffn, w_up, w_down):
    depth = w_in.shape[0]
    bp, tp, d = x_prompt.shape
    bs, ts, _ = x_sample.shape
    hd2 = 2 * HEAD_DIM
    pos_p = jnp.arange(tp, dtype=jnp.int32)
    pos_s = jnp.tile(PAST_LEN + jnp.arange(ts, dtype=jnp.int32), bs)
    xp = x_prompt.reshape(bp * tp, d)
    xs = x_sample.reshape(bs * ts, d)
    kp_l, vp_l, ks_l, vs_l, sg_l = [], [], [], [], []
    for l in range(depth):
        lambda_init = 0.8 - 0.6 * math.exp(-0.3 * l)
        lam_vecs = [a[l].reshape(1, HEAD_DIM) for a in (lambda_q1, lambda_k1, lambda_q2, lambda_k2)]
        sub = subln[l].reshape(1, hd2)
        wts = (norm_mix[l], w_in[l], b_gate[l], q_norm[l], k_norm[l], sub, sgu_norm[l],
               w_s[l], b_s[l], w_o[l], norm_ffn[l], w_up[l], w_down[l])
        ck, cv = cache_k_attn[l], cache_v_attn[l]

        attend_p = lambda q, k, v: _attn_prompt(q, k, v, lam_vecs, sub, lambda_init, bp, min(ATTN_BLOCK, tp))
        attend_s = lambda q, k, v: _attn_sample(q, k, v, ck, cv, lam_vecs, sub, lambda_init, ts)
        w16 = {}
        xp, kp, vp, _ = _layer(xp, pos_p, attend_p, min(tp, SGU_CHUNK), False, PROMPT_TILES, lambda_init, wts, w16)
        decode_tiles = ((bs * ts, DECODE_TN), (bs * ts, DECODE_TN))
        xs, ksm, vsm, sgv = _layer(xs, pos_s, attend_s, min(ts, SGU_CHUNK), True, decode_tiles, lambda_init, wts, w16)
        kp_l.append(kp.reshape(bp, tp, N_HEADS, hd2))
        vp_l.append(vp.reshape(bp, tp, N_HEADS, hd2))
        ks_l.append(ksm.reshape(bs, ts, N_HEADS, hd2))
        vs_l.append(vsm.reshape(bs, ts, N_HEADS, hd2))
        sg_l.append(sgv.reshape(bs, ts, d))
    return (xp.reshape(bp, tp, d), xs.reshape(bs, ts, d), jnp.stack(kp_l), jnp.stack(vp_l),
            jnp.stack(ks_l), jnp.stack(vs_l), jnp.stack(sg_l))
```

```python
import functools
import math

import jax
import jax.numpy as jnp
from jax import lax
from jax.experimental import pallas as pl
from jax.experimental.pallas import tpu as pltpu

D_MODEL = 4096
BATCH = 2
SEQ = 4096
DEC_BATCH = 16
DEC_SEQ = 16
PAST_LEN = 4096
CHUNK = 64
N_HEADS = 16
HEAD_DIM = 128
SGU_CHUNK = 128
SGU_GROUPS = 8
D_FF = 4 * D_MODEL
ROPE_THETA = 10000.0
EPS = 1e-6
NEG_INF = -1e30

LANES = 128
SUBLANES = 8
BF16_ROWS = 16
MXU_WIDTH = 256
EPILOGUE_ROWS = 256
VMEM_LIMIT_BYTES = 60 * 1024 * 1024
MAX_TK = 4096
SAMPLE_CACHE_BLOCK = 512
ATTN_BLOCK = 512
PROMPT_TILES = ((512, 1024), (512, 1024))
FFN_DOWN_TK = 4096
WIDE_ROW_FACTOR = 2
DECODE_TN = 1024
LOG2E = math.log2(math.e)
F32 = jnp.float32
BF16 = jnp.bfloat16


def _params(*sem):
    return pltpu.CompilerParams(dimension_semantics=sem, vmem_limit_bytes=VMEM_LIMIT_BYTES)


def _rmsnorm_kernel(x_ref, g_ref, o_ref):
    x = x_ref[...]
    ms = jnp.mean(x * x, axis=-1, keepdims=True)
    o_ref[...] = (x * lax.rsqrt(ms + EPS) * g_ref[...]).astype(o_ref.dtype)


def _rmsnorm(x, g, tm):
    m, d = x.shape
    return pl.pallas_call(
        _rmsnorm_kernel,
        grid=(m // tm,),
        in_specs=[pl.BlockSpec((tm, d), lambda i: (i, 0)), pl.BlockSpec((1, d), lambda i: (0, 0))],
        out_specs=pl.BlockSpec((tm, d), lambda i: (i, 0)),
        out_shape=jax.ShapeDtypeStruct((m, d), BF16),
        compiler_params=_params("parallel"),
        name="rmsnorm",
    )(x, g.reshape(1, d))


def _mm_kernel(x_ref, w_ref, *refs, nk, n_extra, n_out, epilogue):
    extras, outs = refs[:n_extra], refs[n_extra:n_extra + n_out]
    if nk == 1:
        acc = jnp.dot(x_ref[...], w_ref[...], preferred_element_type=F32)
        epilogue(acc, *extras, *outs)
        return
    acc_ref = refs[n_extra + n_out]
    k = pl.program_id(2)

    @pl.when(k == 0)
    def _():
        acc_ref[...] = jnp.zeros_like(acc_ref)

    acc_ref[...] += jnp.dot(x_ref[...], w_ref[...], preferred_element_type=F32)

    @pl.when(k == nk - 1)
    def _():
        epilogue(acc_ref[...], *extras, *outs)


def _matmul(x, w, col0, ncols, tm, tn, epilogue, extras, outs, name, max_tk=None):
    m, kdim = x.shape
    tk = min(kdim, max_tk or MAX_TK)
    nk = kdim // tk
    assert col0 % tn == 0 and ncols % tn == 0 and m % tm == 0 and kdim % tk == 0
    cb = col0 // tn
    in_specs = [pl.BlockSpec((tm, tk), lambda n, i, k: (i, k)),
                pl.BlockSpec((tk, tn), lambda n, i, k: (k, n + cb))]
    in_specs += [pl.BlockSpec(bs, im) for _, bs, im in extras]
    out_specs = [pl.BlockSpec((tm, tn), lambda n, i, k: (i, n)) for _ in outs]
    out_shape = [jax.ShapeDtypeStruct((m, ncols), dt) for dt in outs]
    scratch = [pltpu.VMEM((tm, tn), F32)] if nk > 1 else []
    res = pl.pallas_call(
        functools.partial(_mm_kernel, nk=nk, n_extra=len(extras), n_out=len(outs), epilogue=epilogue),
        grid=(ncols // tn, m // tm, nk),
        in_specs=in_specs,
        out_specs=out_specs,
        out_shape=out_shape,
        scratch_shapes=scratch,
        compiler_params=_params("parallel", "parallel", "arbitrary"),
        name=name,
    )(x, w, *[a for a, _, _ in extras])
    return res


def _mm_cast_kernel(x_ref, w_ref, *refs, nn, n_extra, n_out, epilogue, ride):
    extras = refs[:n_extra]
    pos = n_extra + (1 if ride else 0)
    outs, w16_ref = refs[pos:pos + n_out], refs[pos + n_out]
    wb_ref = refs[-1]
    n, i = pl.program_id(0), pl.program_id(1)
    chunk_rows = w_ref.shape[0]

    @pl.when(n < nn)
    def _():
        chunk = w_ref[...].astype(BF16)
        wb_ref[n % 2, pl.ds(pl.multiple_of(i * chunk_rows, chunk_rows), chunk_rows), :] = chunk
        w16_ref[...] = chunk

    @pl.when(n > 0)
    def _():
        if ride:
            refs[pos + n_out + 1][...] = refs[n_extra][...].astype(BF16)
        tm = x_ref.shape[0]
        rows = min(EPILOGUE_ROWS, tm)
        w = wb_ref[(n - 1) % 2]
        for r0 in range(0, tm, rows):
            part = [r.at[r0:r0 + rows, :] if r.shape[0] == tm else r for r in (*extras, *outs)]
            acc = jnp.dot(x_ref[r0:r0 + rows, :], w, preferred_element_type=F32)
            epilogue(acc, *part)


def _matmul_cast(x, w, col0, ncols, tm, tn, epilogue, extras, outs, name, ride=None):
    m, kdim = x.shape
    nn, nm = ncols // tn, m // tm
    assert col0 % tn == 0 and ncols % tn == 0 and m % tm == 0 and kdim <= MAX_TK
    assert kdim % (nm * BF16_ROWS) == 0
    cb = col0 // tn
    chunk = kdim // nm

    def row(n, i):
        return jnp.where(n == 0, 0, i)

    def col(n):
        return jnp.maximum(n - 1, 0)

    def loading(n, i):
        return jnp.where(n == nn, nm - 1, i), jnp.minimum(n, nn - 1)

    in_specs = [pl.BlockSpec((tm, kdim), lambda n, i, k: (row(n, i), 0)),
                pl.BlockSpec((chunk, tn), lambda n, i, k: (loading(n, i)[0], loading(n, i)[1] + cb))]
    in_specs += [pl.BlockSpec(bs, functools.partial(lambda n, i, k, im: im(col(n), row(n, i), k), im=im))
                 for _, bs, im in extras]
    out_specs = [pl.BlockSpec((tm, tn), lambda n, i, k: (row(n, i), col(n))) for _ in outs]
    out_specs.append(pl.BlockSpec((chunk, tn), lambda n, i, k: loading(n, i)))
    out_shape = [jax.ShapeDtypeStruct((m, ncols), dt) for dt in outs]
    out_shape.append(jax.ShapeDtypeStruct((kdim, ncols), BF16))
    operands = [x, w] + [a for a, _, _ in extras]
    if ride is not None:
        rrows, rcols = ride.shape
        assert rrows % (nn * nm * BF16_ROWS) == 0
        step_rows = rrows // (nn * nm)
        ride_spec = pl.BlockSpec((step_rows, rcols), lambda n, i, k: (col(n) * nm + row(n, i), 0))
        in_specs.append(ride_spec)
        out_specs.append(ride_spec)
        out_shape.append(jax.ShapeDtypeStruct(ride.shape, BF16))
        operands.append(ride)
    return pl.pallas_call(
        functools.partial(_mm_cast_kernel, nn=nn, n_extra=len(extras), n_out=len(outs), epilogue=epilogue,
                          ride=ride is not None),
        grid=(nn + 1, nm, 1),
        in_specs=in_specs,
        out_specs=out_specs,
        out_shape=out_shape,
        scratch_shapes=[pltpu.VMEM((2, kdim, tn), BF16)],
        compiler_params=_params("arbitrary", "arbitrary", "arbitrary"),
        name=name,
    )(*operands)


def _epi_plain(acc, *outs):
    for o in outs:
        o[...] = acc.astype(o.dtype)


def _epi_norm_rope(acc, g_ref, cos_ref, sin_ref, *outs, scale):
    g = g_ref[...]
    cos = cos_ref[...]
    sin = sin_ref[...]
    width = min(MXU_WIDTH, acc.shape[1])
    row = lax.broadcasted_iota(jnp.int32, (width, width), 0) // HEAD_DIM
    col = lax.broadcasted_iota(jnp.int32, (width, width), 1) // HEAD_DIM
    same_head = jnp.where(row == col, 1.0, 0.0).astype(BF16)
    for c0 in range(0, acc.shape[1], width):
        z = acc[:, c0:c0 + width]
        ss = jnp.dot((z * z).astype(BF16), same_head, preferred_element_type=F32)
        normed = z * lax.rsqrt(ss * (1.0 / HEAD_DIM) + EPS)
        for j in range(0, width, HEAD_DIM):
            y = normed[:, j:j + HEAD_DIM] * g
            y = y * cos + pltpu.roll(y, HEAD_DIM // 2, axis=1) * sin
            if scale != 1.0:
                y = y * scale
            for o in outs:
                o[:, c0 + j:c0 + j + HEAD_DIM] = y.astype(o.dtype)


def _epi_gate(acc, b_ref, o_ref):
    o_ref[...] = jax.nn.sigmoid(acc + b_ref[...]).astype(o_ref.dtype)


def _epi_residual(acc, r_ref, o_ref):
    o_ref[...] = (r_ref[...] + acc).astype(o_ref.dtype)


def _epi_relu2(acc, o_ref):
    o_ref[...] = jnp.square(jnp.maximum(acc, 0.0)).astype(o_ref.dtype)


def _rope_tables(pos):
    half = HEAD_DIM // 2
    inv = ROPE_THETA ** (-jnp.arange(half, dtype=F32) / half)
    ang = pos.astype(F32)[:, None] * inv[None, :]
    cos, sin = jnp.cos(ang), jnp.sin(ang)
    return jnp.concatenate([cos, cos], axis=-1), jnp.concatenate([-sin, sin], axis=-1)


def _lambda(lq1, lk1, lq2, lk2, lambda_init):
    s1 = jnp.sum(lq1[...] * lk1[...], axis=-1, keepdims=True)
    s2 = jnp.sum(lq2[...] * lk2[...], axis=-1, keepdims=True)
    return jnp.exp(s1) - jnp.exp(s2) + lambda_init


def _subln(o, subln_ref, lambda_init):
    ms = jnp.mean(o * o, axis=-1, keepdims=True)
    return o * lax.rsqrt(ms + EPS) * subln_ref[...] * (1.0 - lambda_init)


def _for_each_block(count, fn):
    def pair(t, carry):
        fn(2 * t, 2 * t + 1)
        return carry

    lax.fori_loop(0, count // 2, pair, 0)

    @pl.when(count % 2 == 1)
    def _():
        fn(count - 1)


def _attn_prompt_kernel(lq1, lk1, lq2, lk2, subln_ref, bias_ref, q_ref, k_ref, v_ref, o_ref,
                        s_ref, acc_ref, mp_ref, mb_ref, lp_ref, *, tq, lambda_init):
    qi = pl.program_id(2)
    d = HEAD_DIM
    nl = tq // LANES
    nt = (((1,), (1,)), ((), ()))
    mp_ref[...] = jnp.full_like(mp_ref, NEG_INF)

    def scores(blocks, masked):
        for c in range(2):
            m = mp_ref[c]
            for j in blocks:
                off = pl.multiple_of(j * tq, tq)
                s = lax.dot_general(q_ref[:, c * d:(c + 1) * d], k_ref[pl.ds(off, tq), c * d:(c + 1) * d], nt,
                                    preferred_element_type=F32)
                if masked:
                    s = s + bias_ref[...]
                s_ref[c, j] = s
                for t in range(nl):
                    m = jnp.maximum(m, s[:, t * LANES:(t + 1) * LANES])
            mp_ref[c] = m

    _for_each_block(qi, lambda *blocks: scores(blocks, False))
    scores((qi,), True)

    for c in range(2):
        mb_ref[c] = jnp.broadcast_to(jnp.max(mp_ref[c], axis=-1, keepdims=True), (tq, LANES))
    lp_ref[...] = jnp.zeros_like(lp_ref)
    acc_ref[...] = jnp.zeros_like(acc_ref)

    def accumulate(*blocks):
        for c in range(2):
            mb = mb_ref[c]
            lp = lp_ref[c]
            for j in blocks:
                s = s_ref[c, j]
                ps = [jnp.exp2(s[:, t * LANES:(t + 1) * LANES] - mb) for t in range(nl)]
                for p in ps:
                    lp = lp + p
                pb = jnp.concatenate(ps, axis=1).astype(BF16)
                acc_ref[c] += jnp.dot(pb, v_ref[pl.ds(pl.multiple_of(j * tq, tq), tq), :],
                                      preferred_element_type=F32)
            lp_ref[c] = lp

    _for_each_block(qi + 1, accumulate)

    lam = _lambda(lq1, lk1, lq2, lk2, lambda_init)
    l0 = jnp.sum(lp_ref[0], axis=-1, keepdims=True)
    l1 = jnp.sum(lp_ref[1], axis=-1, keepdims=True)
    o = acc_ref[0] / l0 - lam * (acc_ref[1] / l1)
    o_ref[...] = _subln(o, subln_ref, lambda_init).astype(o_ref.dtype)


def _attn_prompt(q, k, v, lam_vecs, subln, lambda_init, batch, tq):
    mt, width = q.shape
    t = mt // batch
    hd2 = 2 * HEAD_DIM
    nh = width // hd2
    nq = t // tq
    chunk_of = jnp.arange(tq, dtype=jnp.int32) // CHUNK
    bias = jnp.where(chunk_of[None, :] <= chunk_of[:, None], 0.0, NEG_INF).astype(F32)
    vec = pl.BlockSpec((1, HEAD_DIM), lambda b, h, i: (0, 0))
    return pl.pallas_call(
        functools.partial(_attn_prompt_kernel, tq=tq, lambda_init=lambda_init),
        grid=(batch, nh, nq),
        in_specs=[vec, vec, vec, vec,
                  pl.BlockSpec((1, hd2), lambda b, h, i: (0, 0)),
                  pl.BlockSpec((tq, tq), lambda b, h, i: (0, 0)),
                  pl.BlockSpec((tq, hd2), lambda b, h, i: (b * nq + i, h)),
                  pl.BlockSpec((t, hd2), lambda b, h, i: (b, h)),
                  pl.BlockSpec((t, hd2), lambda b, h, i: (b, h))],
        out_specs=pl.BlockSpec((tq, hd2), lambda b, h, i: (b * nq + i, h)),
        out_shape=jax.ShapeDtypeStruct((mt, width), BF16),
        scratch_shapes=[pltpu.VMEM((2, nq, tq, tq), F32), pltpu.VMEM((2, tq, hd2), F32),
                        pltpu.VMEM((2, tq, LANES), F32), pltpu.VMEM((2, tq, LANES), F32),
                        pltpu.VMEM((2, tq, LANES), F32)],
        compiler_params=_params("parallel", "parallel", "arbitrary"),
        name="attn_prompt",
    )(*lam_vecs, subln, bias, q, k, v)


def _transpose_sublanes(x):
    v = list(x)
    sub = lax.broadcasted_iota(jnp.int32, (SUBLANES, LANES), 0)
    s = SUBLANES // 2
    while s:
        low = (sub & s) == 0
        nxt = list(v)
        for i in range(SUBLANES):
            if not i & s:
                a, b = v[i], v[i + s]
                nxt[i] = jnp.where(low, a, pltpu.roll(b, s, axis=0))
                nxt[i + s] = jnp.where(low, pltpu.roll(a, SUBLANES - s, axis=0), b)
        v = nxt
        s //= 2
    return v


def _heads_to_rows(src_refs, dst_ref, rows):
    tp, ngroups = src_refs[0].shape[:2]
    assert rows % SUBLANES == 0

    def body(t, carry):
        p0 = pl.multiple_of(t * rows, rows)
        for c, ref in enumerate(src_refs):
            for hh in range(ngroups):
                x = ref[pl.ds(p0, rows), hh]
                parts = [_transpose_sublanes([x[g + i] for i in range(SUBLANES)])
                         for g in range(0, rows, SUBLANES)]
                for j in range(SUBLANES):
                    tile = jnp.concatenate([part[j] for part in parts], axis=0).astype(dst_ref.dtype)
                    dst_ref[hh * SUBLANES + j, pl.ds(p0, rows), c * LANES:(c + 1) * LANES] = tile
        return carry

    lax.fori_loop(0, tp // rows, body, 0)


def _attn_sample_kernel(lq1, lk1, lq2, lk2, subln_ref, q_ref, kn_ref, vn_ref, k0_ref, k1_ref, v0_ref, v1_ref,
                        o_ref, m_ref, l_ref, acc_ref, kt_ref, vt_ref, s_ref, p_ref,
                        *, past_len, nheads, lambda_init):
    d = HEAD_DIM
    tq = q_ref.shape[0]
    tp = k0_ref.shape[0]
    pc = pl.program_id(1)
    nt = (((1,), (1,)), ((), ()))

    @pl.when(pc == 0)
    def _():
        m_ref[...] = jnp.full_like(m_ref, NEG_INF)
        l_ref[...] = jnp.zeros_like(l_ref)
        acc_ref[...] = jnp.zeros_like(acc_ref)

    def stacked_q(h):
        q = q_ref[:, h * 2 * d:(h + 1) * 2 * d]
        z = jnp.zeros((tq, d), q.dtype)
        return jnp.concatenate([jnp.concatenate([q[:, :d], z], axis=1),
                                jnp.concatenate([z, q[:, d:]], axis=1)], axis=0)

    def qchunk(shape):
        return (past_len + lax.broadcasted_iota(jnp.int32, shape, 0) % tq) // CHUNK

    def softmax_step(s, valid):
        m_prev = m_ref[...]
        m_new = jnp.maximum(m_prev, jnp.max(s, axis=-1, keepdims=True))
        alpha = jnp.exp2(m_prev - m_new)
        p = jnp.exp2(s - m_new)
        if valid is not None:
            p = jnp.where(valid, p, 0.0)
        l_ref[...] = alpha * l_ref[...] + jnp.sum(p, axis=-1, keepdims=True)
        m_ref[...] = m_new
        acc_ref[...] = alpha * acc_ref[...]
        return p.astype(BF16)

    _heads_to_rows((k0_ref, k1_ref), kt_ref, BF16_ROWS)
    _heads_to_rows((v0_ref, v1_ref), vt_ref, BF16_ROWS)

    rows = 2 * tq
    for h in range(nheads):
        s_ref[h * rows:(h + 1) * rows, :] = lax.dot_general(stacked_q(h), kt_ref[h], nt,
                                                            preferred_element_type=F32)
    shape = (nheads * rows, tp)
    vis = (pc * tp + lax.broadcasted_iota(jnp.int32, shape, 1)) // CHUNK <= qchunk(shape)
    p_ref[...] = softmax_step(jnp.where(vis, s_ref[...], NEG_INF), None)
    for h in range(nheads):
        sl = slice(h * rows, (h + 1) * rows)
        acc_ref[sl, :] += jnp.dot(p_ref[sl, :], vt_ref[h], preferred_element_type=F32)

    @pl.when(pc == pl.num_programs(1) - 1)
    def _():
        lam = _lambda(lq1, lk1, lq2, lk2, lambda_init)
        pad = jnp.zeros((LANES - tq, 2 * d), kn_ref.dtype)
        vn = []
        for h in range(nheads):
            hs = slice(h * 2 * d, (h + 1) * 2 * d)
            kn = jnp.concatenate([kn_ref[:, hs], pad], axis=0)
            vn.append(jnp.concatenate([vn_ref[:, hs], pad], axis=0))
            s_ref[h * rows:(h + 1) * rows, :LANES] = lax.dot_general(stacked_q(h), kn, nt,
                                                                     preferred_element_type=F32)
        shape_n = (nheads * rows, LANES)
        col = lax.broadcasted_iota(jnp.int32, shape_n, 1)
        valid = col < tq
        vis_n = valid & ((past_len + col) // CHUNK <= qchunk(shape_n))
        p_n = softmax_step(jnp.where(vis_n, s_ref[:, :LANES], NEG_INF), valid)
        for h in range(nheads):
            sl = slice(h * rows, (h + 1) * rows)
            o2 = (acc_ref[sl, :] + jnp.dot(p_n[sl, :], vn[h], preferred_element_type=F32)) / l_ref[sl, :]
            o = o2[:tq] - lam * o2[tq:]
            o_ref[:, h * 2 * d:(h + 1) * 2 * d] = _subln(o, subln_ref, lambda_init).astype(o_ref.dtype)


def _attn_sample(q, kn, vn, cache_k, cache_v, lam_vecs, subln, lambda_init, tq):
    mt, width = q.shape
    nb, past_len, nh, hd2 = cache_k.shape
    tp = min(SAMPLE_CACHE_BLOCK, past_len)
    assert nh % SUBLANES == 0 and tp % SUBLANES == 0 and past_len % tp == 0
    cache_k = cache_k.reshape(nb, past_len, nh // SUBLANES, SUBLANES, hd2)
    cache_v = cache_v.reshape(nb, past_len, nh // SUBLANES, SUBLANES, hd2)
    vec = pl.BlockSpec((1, HEAD_DIM), lambda b, p: (0, 0))
    row = pl.BlockSpec((tq, width), lambda b, p: (b, 0))
    half = [pl.BlockSpec((None, tp, nh // SUBLANES, SUBLANES, HEAD_DIM),
                         functools.partial(lambda b, p, c: (b, p, 0, 0, c), c=c)) for c in range(2)]
    return pl.pallas_call(
        functools.partial(_attn_sample_kernel, past_len=past_len, nheads=nh, lambda_init=lambda_init),
        grid=(nb, past_len // tp),
        in_specs=[vec, vec, vec, vec, pl.BlockSpec((1, hd2), lambda b, p: (0, 0)),
                  row, row, row, half[0], half[1], half[0], half[1]],
        out_specs=row,
        out_shape=jax.ShapeDtypeStruct((mt, width), BF16),
        scratch_shapes=[pltpu.VMEM((nh * 2 * tq, 1), F32), pltpu.VMEM((nh * 2 * tq, 1), F32),
                        pltpu.VMEM((nh * 2 * tq, hd2), F32),
                        pltpu.VMEM((nh, tp, hd2), BF16), pltpu.VMEM((nh, tp, hd2), BF16),
                        pltpu.VMEM((nh * 2 * tq, tp), F32), pltpu.VMEM((nh * 2 * tq, tp), BF16)],
        compiler_params=_params("parallel", "arbitrary"),
        name="attn_sample",
    )(*lam_vecs, subln, q, kn, vn, cache_k, cache_k, cache_v, cache_v)


def _sgu_merge_kernel(vs_ref, u_ref, ga_ref, gb_ref, oa_ref, gn_ref, w_ref, bt_ref, *outs,
                      seg, groups, want_v):
    vs = vs_ref[...].astype(F32)
    ms = jnp.mean(vs * vs, axis=-1, keepdims=True)
    vn = vs * lax.rsqrt(ms + EPS) * gn_ref[...]
    if want_v:
        outs[1][...] = vn
    rows = vs.shape[0]
    gd = vs.shape[1] // groups
    r = lax.broadcasted_iota(jnp.int32, (rows, rows), 0)
    c = lax.broadcasted_iota(jnp.int32, (rows, rows), 1)
    keep = (r // seg == c // seg) & (c <= r)
    vb = vn.astype(BF16)
    for g in range(groups):
        sl = slice(g * gd, (g + 1) * gd)
        w = jnp.where(keep, w_ref[g], 0.0).astype(BF16)
        s = jnp.dot(w, vb[:, sl], preferred_element_type=F32) + bt_ref[:, g:g + 1]
        o_sgu = u_ref[:, sl].astype(F32) * s
        merged = (ga_ref[:, sl].astype(F32) * oa_ref[:, sl].astype(F32)
                  + gb_ref[:, sl].astype(F32) * o_sgu)
        outs[0][:, sl] = merged.astype(outs[0].dtype)


def _sgu_merge(vs, u, gate, o_attn, sgu_norm, w_s, b_s, seg, want_v):
    m, d = vs.shape
    rows = SGU_CHUNK
    groups = w_s.shape[0]
    rep = rows // seg
    w_t = jnp.tile(w_s[:, :seg, :seg], (1, rep, rep))
    b_t = jnp.tile(jnp.swapaxes(b_s[:, :seg], 0, 1), (rep, 1))
    blk = pl.BlockSpec((rows, d), lambda i: (i, 0))
    out_shape = [jax.ShapeDtypeStruct((m, d), BF16)]
    if want_v:
        out_shape.append(jax.ShapeDtypeStruct((m, d), F32))
    return pl.pallas_call(
        functools.partial(_sgu_merge_kernel, seg=seg, groups=groups, want_v=want_v),
        grid=(m // rows,),
        in_specs=[blk, blk, blk, pl.BlockSpec((rows, d), lambda i: (i, 1)), blk,
                  pl.BlockSpec((1, d), lambda i: (0, 0)),
                  pl.BlockSpec((groups, rows, rows), lambda i: (0, 0, 0)),
                  pl.BlockSpec((rows, groups), lambda i: (0, 0))],
        out_specs=[blk] * len(out_shape),
        out_shape=out_shape,
        compiler_params=_params("parallel"),
        name="sgu_merge",
    )(vs, u, gate, gate, o_attn, sgu_norm.reshape(1, d), w_t, b_t)


def _layer(x, pos, attend, seg, want_v, tiles, lambda_init, wts, w16):
    (norm_mix, w_in, b_gate, q_norm, k_norm, subln, sgu_norm, w_s, b_s, w_o, norm_ffn, w_up, w_down) = wts
    m, d = x.shape
    qk_w = N_HEADS * 2 * HEAD_DIM
    (tm, tn), (tm_down, tn_down) = tiles
    tn, tn_down = min(tn, d), min(tn_down, d)
    rounding = not w16
    cos, sin = _rope_tables(pos)

    def mm(name, a, w, col0, ncols, epilogue, extras, outs, ride=None, wide=False):
        rows = min(tm * WIDE_ROW_FACTOR, m) if wide else tm
        if not rounding:
            return _matmul(a, w16[name], 0, ncols, rows, tn, epilogue, extras, outs, name)
        res = _matmul_cast(a, w, col0, ncols, rows, tn, epilogue, extras, outs, name,
                           None if ride is None else ride[1])
        w16[name] = res[len(outs)]
        if ride is not None:
            w16[ride[0]] = res[len(outs) + 1]
        return res[:len(outs)]

    def rope_extras(g, rows):
        npos = pos.shape[0] // rows
        return [(g.reshape(1, HEAD_DIM), (1, HEAD_DIM), lambda n, i, k: (0, 0)),
                (cos, (rows, HEAD_DIM), lambda n, i, k: (i % npos, 0)),
                (sin, (rows, HEAD_DIM), lambda n, i, k: (i % npos, 0))]

    def res_extras(a, rows, cols):
        return [(a, (rows, cols), lambda n, i, k: (i, n))]

    h = _rmsnorm(x, norm_mix, min(tm, 256))
    (q,) = mm("proj_q", h, w_in, 0, qk_w, functools.partial(_epi_norm_rope, scale=HEAD_DIM ** -0.5 * LOG2E),
              rope_extras(q_norm, min(tm * WIDE_ROW_FACTOR, m)), [BF16], wide=True)
    k32, kb = mm("proj_k", h, w_in, qk_w, qk_w, functools.partial(_epi_norm_rope, scale=1.0),
                 rope_extras(k_norm, tm), [F32, BF16])
    v32, vb = mm("proj_v", h, w_in, 2 * qk_w, qk_w, _epi_plain, [], [F32, BF16])
    (u,) = mm("proj_u", h, w_in, 3 * qk_w, d, _epi_plain, [], [BF16], wide=True)
    (vs,) = mm("proj_vs", h, w_in, 3 * qk_w + d, d, _epi_plain, [], [F32 if want_v else BF16], wide=True)
    (gate,) = mm("proj_gate", h, w_in, 3 * qk_w + 2 * d, 2 * d, _epi_gate,
                 [(b_gate.reshape(1, 2 * d), (1, tn), lambda n, i, k: (0, n))], [BF16], wide=True)

    o_attn = attend(q, kb, vb)
    res = _sgu_merge(vs, u, gate, o_attn, sgu_norm, w_s, b_s, seg, want_v)
    merged = res[0]
    (x1,) = mm("proj_out", merged, w_o, 0, d, _epi_residual, res_extras(x, tm, tn), [F32])
    hf = _rmsnorm(x1, norm_ffn, min(tm, 256))
    (hid,) = mm("ffn_up", hf, w_up, 0, w_up.shape[1], _epi_relu2, [], [BF16], ride=("ffn_down", w_down),
                wide=True)
    (y,) = _matmul(hid, w16["ffn_down"], 0, d, tm_down, tn_down, _epi_residual,
                   res_extras(x1, tm_down, tn_down), [F32], "ffn_down", max_tk=FFN_DOWN_TK)
    return y, k32, v32, (res[1] if want_v else None)


def kernel(x_prompt, x_sample, cache_k_attn, cache_v_attn, norm_mix, w_in, b_gate, q_norm, k_norm, lambda_q1, lambda_k1, lambda_q2, lambda_k2, subln, sgu_norm, w_s, b_s, w_o, norm_ffn, w_up, w_down):
    depth = w_in.shape[0]
    bp, tp, d = x_prompt.shape
    bs, ts, _ = x_sample.shape
    hd2 = 2 * HEAD_DIM
    pos_p = jnp.arange(tp, dtype=jnp.int32)
    pos_s = jnp.tile(PAST_LEN + jnp.arange(ts, dtype=jnp.int32), bs)
    xp = x_prompt.reshape(bp * tp, d)
    xs = x_sample.reshape(bs * ts, d)
    kp_l, vp_l, ks_l, vs_l, sg_l = [], [], [], [], []
    for l in range(depth):
        lambda_init = 0.8 - 0.6 * math.exp(-0.3 * l)
        lam_vecs = [a[l].reshape(1, HEAD_DIM) for a in (lambda_q1, lambda_k1, lambda_q2, lambda_k2)]
        sub = subln[l].reshape(1, hd2)
        wts = (norm_mix[l], w_in[l], b_gate[l], q_norm[l], k_norm[l], sub, sgu_norm[l],
               w_s[l], b_s[l], w_o[l], norm_ffn[l], w_up[l], w_down[l])
        ck, cv = cache_k_attn[l], cache_v_attn[l]

        attend_p = lambda q, k, v: _attn_prompt(q, k, v, lam_vecs, sub, lambda_init, bp, min(ATTN_BLOCK, tp))
        attend_s = lambda q, k, v: _attn_sample(q, k, v, ck, cv, lam_vecs, sub, lambda_init, ts)
        w16 = {}
        xp, kp, vp, _ = _layer(xp, pos_p, attend_p, min(tp, SGU_CHUNK), False, PROMPT_TILES, lambda_init, wts, w16)
        decode_tiles = ((bs * ts, DECODE_TN), (bs * ts, DECODE_TN))
        xs, ksm, vsm, sgv = _layer(xs, pos_s, attend_s, min(ts, SGU_CHUNK), True, decode_tiles, lambda_init, wts, w16)
        kp_l.append(kp.reshape(bp, tp, N_HEADS, hd2))
        vp_l.append(vp.reshape(bp, tp, N_HEADS, hd2))
        ks_l.append(ksm.reshape(bs, ts, N_HEADS, hd2))
        vs_l.append(vsm.reshape(bs, ts, N_HEADS, hd2))
        sg_l.append(sgv.reshape(bs, ts, d))
    return (xp.reshape(bp, tp, d), xs.reshape(bs, ts, d), jnp.stack(kp_l), jnp.stack(vp_l),
            jnp.stack(ks_l), jnp.stack(vs_l), jnp.stack(sg_l))
```
